```python
import math
import jax, jax.numpy as jnp
from jax import lax
import numpy as np

D_MODEL = 1024
BATCH = 1
SEQ = 16384
DEPTH = 4
DEC_BATCH = 16
DEC_SEQ = 16
PAST_LEN = 4096

CHUNK = 64
N_PREV_CHUNKS = 8
BAND_PAST = N_PREV_CHUNKS * CHUNK
BAND = BAND_PAST + CHUNK
N_HEADS = 16
HEAD_DIM = D_MODEL // N_HEADS
D_FF = 4 * D_MODEL
CONV_W = 3
REL_MAX = 128
N_A = DEPTH // 2
N_B = DEPTH - N_A
ALPHA = (2.0 * DEPTH) ** 0.25
BETA = (8.0 * DEPTH) ** -0.25
LN_EPS = 1e-5

kernel_name = "yoco_shortconv_chunkband_deepnorm_step"


def _layer_norm(x, g, b):
    xf = x.astype(jnp.float32)
    mu = jnp.mean(xf, axis=-1, keepdims=True)
    var = jnp.mean(jnp.square(xf - mu), axis=-1, keepdims=True)
    y = (xf - mu) * lax.rsqrt(var + LN_EPS) * g.astype(jnp.float32) + b.astype(jnp.float32)
    return y.astype(x.dtype)


def _sq_relu_mlp(x, w_up, w_down):
    return jnp.square(jax.nn.relu(x @ w_up)) @ w_down


def _short_conv_mixer(x, prev, w_in, conv_w, w_out):
    t = x.shape[1]
    b_gate, c_gate, h = jnp.split(x @ w_in, 3, axis=-1)
    z = c_gate * h
    zp = jnp.concatenate([prev.astype(z.dtype), z], axis=1)
    y = conv_w[0] * zp[:, 0:t]
    for j in range(1, CONV_W):
        y = y + conv_w[j] * zp[:, j:j + t]
    return (b_gate * y) @ w_out, zp[:, -(CONV_W - 1):]


def _rel_bias(table, rel):
    idx = jnp.clip(rel, -REL_MAX, REL_MAX) + REL_MAX
    return jnp.transpose(table[idx], (2, 0, 1))


def _attend(q, k, v, bias, valid):
    s = jnp.einsum('...qhd,...khd->...hqk', q, k).astype(jnp.float32) * (HEAD_DIM ** -0.5)
    s = s + bias.astype(jnp.float32)
    if valid is not None:
        s = jnp.where(valid[..., None, None, :], s, -jnp.inf)
    p = jax.nn.softmax(s, axis=-1).astype(v.dtype)
    return jnp.einsum('...hqk,...khd->...qhd', p, v)


def _band_attention_prompt(q, k, v, table):
    b, s = q.shape[0], q.shape[1]
    nc = s // CHUNK
    pad = ((0, 0), (BAND_PAST, 0), (0, 0), (0, 0))
    kp, vp = jnp.pad(k, pad), jnp.pad(v, pad)
    idx = jnp.arange(nc)[:, None] * CHUNK + jnp.arange(BAND)[None, :]
    kb, vb = kp[:, idx], vp[:, idx]
    valid = idx >= BAND_PAST
    qc = q.reshape(b, nc, CHUNK, N_HEADS, HEAD_DIM)
    rel = jnp.arange(CHUNK)[:, None] + BAND_PAST - jnp.arange(BAND)[None, :]
    o = _attend(qc, kb, vb, _rel_bias(table, rel), valid)
    return o.reshape(b, s, D_MODEL)


def _band_attention_step(q, k_new, v_new, k_cache, v_cache, table):
    b, t = q.shape[0], q.shape[1]
    w = k_cache.shape[1]
    k = jnp.concatenate([k_cache.astype(k_new.dtype), k_new], axis=1)
    v = jnp.concatenate([v_cache.astype(v_new.dtype), v_new], axis=1)
    rel = w + jnp.arange(t)[:, None] - jnp.arange(w + t)[None, :]
    o = _attend(q, k, v, _rel_bias(table, rel), None)
    return o.reshape(b, t, D_MODEL)


def _trunk(x, conv_prev, k_cache, v_cache, ln_mix_g, ln_mix_b, ln_ffn_g, ln_ffn_b, w_up, w_down,
           w_in_a, conv_w_a, w_out_a, w_k, w_v, w_q_b, w_o_b, rel_bias_b):
    b, t = x.shape[0], x.shape[1]
    conv_states = []
    k = v = None
    for l in range(DEPTH):
        if l < N_A:
            h, st = _short_conv_mixer(x, conv_prev[l], w_in_a[l], conv_w_a[l], w_out_a[l])
            conv_states.append(st)
        else:
            if l == N_A:
                k = (x @ w_k).reshape(b, t, N_HEADS, HEAD_DIM)
                v = (x @ w_v).reshape(b, t, N_HEADS, HEAD_DIM)
            j = l - N_A
            q = (x @ w_q_b[j]).reshape(b, t, N_HEADS, HEAD_DIM)
            if k_cache is None:
                o = _band_attention_prompt(q, k, v, rel_bias_b[j])
            else:
                o = _band_attention_step(q, k, v, k_cache, v_cache, rel_bias_b[j])
            h = o @ w_o_b[j]
        x = _layer_norm(ALPHA * x + h, ln_mix_g[l], ln_mix_b[l])
        x = _layer_norm(ALPHA * x + _sq_relu_mlp(x, w_up[l], w_down[l]), ln_ffn_g[l], ln_ffn_b[l])
    return x, jnp.stack(conv_states, axis=0), k, v


def setup_inputs(seed: int = 0) -> dict:
    key = jax.random.key(seed)
    ks = jax.random.split(key, 20)
    f32 = jnp.float32
    d = D_MODEL
    w_cache = min(BAND_PAST, PAST_LEN)
    nrm = lambda k, shape, s: jax.random.normal(k, shape, f32) * s
    return {
        "x_prompt": nrm(ks[0], (BATCH, SEQ, d), 1.0),
        "x_sample": nrm(ks[1], (DEC_BATCH, DEC_SEQ, d), 1.0),
        "cache_conv": nrm(ks[2], (N_A, DEC_BATCH, CONV_W - 1, d), 1.0),
        "cache_k": nrm(ks[3], (DEC_BATCH, w_cache, N_HEADS, HEAD_DIM), 1.0),
        "cache_v": nrm(ks[4], (DEC_BATCH, w_cache, N_HEADS, HEAD_DIM), BETA),
        "ln_mix_g": 1.0 + nrm(ks[5], (DEPTH, d), 0.02),
        "ln_mix_b": nrm(ks[6], (DEPTH, d), 0.02),
        "ln_ffn_g": 1.0 + nrm(ks[7], (DEPTH, d), 0.02),
        "ln_ffn_b": nrm(ks[8], (DEPTH, d), 0.02),
        "w_up": nrm(ks[9], (DEPTH, d, D_FF), d ** -0.5),
        "w_down": nrm(ks[10], (DEPTH, D_FF, d), BETA * D_FF ** -0.5),
        "w_in_a": nrm(ks[11], (N_A, d, 3 * d), d ** -0.5),
        "conv_w_a": nrm(ks[12], (N_A, CONV_W, d), CONV_W ** -0.5),
        "w_out_a": nrm(ks[13], (N_A, d, d), BETA * d ** -0.5),
        "w_k": nrm(ks[14], (d, d), d ** -0.5),
        "w_v": nrm(ks[15], (d, d), BETA * d ** -0.5),
        "w_q_b": nrm(ks[16], (N_B, d, d), d ** -0.5),
        "w_o_b": nrm(ks[17], (N_B, d, d), BETA * d ** -0.5),
        "rel_bias_b": nrm(ks[18], (N_B, 2 * REL_MAX + 1, N_HEADS), 0.5),
    }


def reference(x_prompt, x_sample, cache_conv, cache_k, cache_v, ln_mix_g, ln_mix_b, ln_ffn_g,
              ln_ffn_b, w_up, w_down, w_in_a, conv_w_a, w_out_a, w_k, w_v, w_q_b, w_o_b, rel_bias_b):
    conv_zero = jnp.zeros((N_A, x_prompt.shape[0], CONV_W - 1, D_MODEL), x_prompt.dtype)
    y_prompt, conv_prompt, k_p, v_p = _trunk(
        x_prompt, conv_zero, None, None, ln_mix_g=ln_mix_g, ln_mix_b=ln_mix_b, ln_ffn_g=ln_ffn_g,
        ln_ffn_b=ln_ffn_b, w_up=w_up, w_down=w_down, w_in_a=w_in_a, conv_w_a=conv_w_a,
        w_out_a=w_out_a, w_k=w_k, w_v=w_v, w_q_b=w_q_b, w_o_b=w_o_b, rel_bias_b=rel_bias_b)
    y_sample, conv_sample, k_sample, v_sample = _trunk(
        x_sample, cache_conv, cache_k, cache_v, ln_mix_g=ln_mix_g, ln_mix_b=ln_mix_b,
        ln_ffn_g=ln_ffn_g, ln_ffn_b=ln_ffn_b, w_up=w_up, w_down=w_down, w_in_a=w_in_a,
        conv_w_a=conv_w_a, w_out_a=w_out_a, w_k=w_k, w_v=w_v, w_q_b=w_q_b, w_o_b=w_o_b,
        rel_bias_b=rel_bias_b)
    keep = min(BAND_PAST, x_prompt.shape[1])
    k_prompt = k_p[:, -keep:]
    v_prompt = v_p[:, -keep:]
    return (y_prompt, y_sample, conv_prompt, k_prompt, v_prompt, conv_sample, k_sample, v_sample)
```

```python
import dataclasses
import functools

import jax
import jax.numpy as jnp
from jax import lax
from jax.experimental import pallas as pl
from jax.experimental.pallas import tpu as pltpu

D_MODEL = 1024
N_HEADS = 16
HEAD_DIM = 64
D_FF = 4 * D_MODEL
DEPTH = 4
N_A = 2
CHUNK = 64
N_PREV_CHUNKS = 8
BAND_PAST = N_PREV_CHUNKS * CHUNK
REL_MAX = 128
ALPHA = (2.0 * DEPTH) ** 0.25
LN_EPS = 1e-5

LANES_V7X = 128
SUBLANES_V7X = 8
VMEM_BYTES_V7X = 64 * 1024 * 1024
VMEM_LIMIT_BYTES = VMEM_BYTES_V7X - 8 * 1024 * 1024

PAIR_W = LANES_V7X
N_PAIRS = D_MODEL // PAIR_W
Q_BLOCK = 4 * CHUNK
K_BLOCK = BAND_PAST + Q_BLOCK
ATT_STEP = 2 * Q_BLOCK
MASK_VALUE = -1e30

BF16 = jnp.bfloat16
F32 = jnp.float32


def _const_spec(shape):
    zeros = (0,) * len(shape)
    return pl.BlockSpec(shape, lambda i: zeros, pipeline_mode=pl.Buffered(1))


def _dot(a, b):
    return jnp.dot(a, b, preferred_element_type=F32)


def _dot_nt(a, b):
    return lax.dot_general(a, b, (((1,), (1,)), ((), ())), preferred_element_type=F32)


def _layer_norm(x, g, b):
    mu = jnp.mean(x, axis=-1, keepdims=True)
    xc = x - mu
    var = jnp.mean(xc * xc, axis=-1, keepdims=True)
    return xc * lax.rsqrt(var + LN_EPS) * g + b


def _pairs_to_rows(ref):
    return jnp.concatenate([ref[p] for p in range(N_PAIRS)], axis=-1)


def _store_pairs(ref, val):
    for p in range(N_PAIRS):
        ref[p] = val[:, p * PAIR_W:(p + 1) * PAIR_W].astype(BF16)


@dataclasses.dataclass(frozen=True)
class _LayerCfg:
    conv: bool
    stream_len: int
    tm: int

    @property
    def streams(self):
        return self.stream_len > 0


def _layer_kernel(*refs, cfg):
    it = iter(refs)
    x_ref = next(it)
    if cfg.conv:
        if cfg.streams:
            p1_ref, p2_ref = next(it), next(it)
        w_in_ref, cw_ref, w_mix_ref = next(it), next(it), next(it)
    else:
        a_ref, w_mix_ref = next(it), next(it)
    g1_ref, b1_ref, g2_ref, b2_ref = next(it), next(it), next(it), next(it)
    w_up_ref, w_down_ref = next(it), next(it)
    y_ref = next(it)
    if cfg.conv:
        ztail_ref = next(it)
        if not cfg.streams:
            carry_ref = next(it)

    tm = cfg.tm
    x = x_ref[...]
    if cfg.conv:
        bch = _dot(x.astype(BF16), w_in_ref[...])
        b_gate = bch[:, :D_MODEL]
        z = bch[:, D_MODEL:2 * D_MODEL] * bch[:, 2 * D_MODEL:]
        r1 = pltpu.roll(z, 1, 0)
        r2 = pltpu.roll(z, 2, 0)
        row = lax.broadcasted_iota(jnp.int32, (tm, D_MODEL), 0)
        if cfg.streams:
            rm = row & (cfg.stream_len - 1)
            zm1 = jnp.where(rm == 0, p1_ref[...], r1)
            zm2 = jnp.where(rm < 2, p2_ref[...], r2)
            ztail_ref[...] = z
        else:
            @pl.when(pl.program_id(0) == 0)
            def _():
                carry_ref[...] = jnp.zeros_like(carry_ref)
            prev = carry_ref[...]
            pm1 = prev[SUBLANES_V7X - 1:SUBLANES_V7X, :]
            pm2 = prev[SUBLANES_V7X - 2:SUBLANES_V7X - 1, :]
            zm1 = jnp.where(row == 0, pm1, r1)
            zm2 = jnp.where(row == 0, pm2, jnp.where(row == 1, pm1, r2))
            tail = z[tm - SUBLANES_V7X:, :]
            carry_ref[...] = tail
            ztail_ref[...] = tail
        cw = cw_ref[...]
        yc = cw[0:1, :] * zm2 + cw[1:2, :] * zm1 + cw[2:3, :] * z
        mix_in = (b_gate * yc).astype(BF16)
    else:
        mix_in = _pairs_to_rows(a_ref)
    h = _dot(mix_in, w_mix_ref[...])
    x1 = _layer_norm(ALPHA * x + h, g1_ref[...], b1_ref[...])
    u = _dot(x1.astype(BF16), w_up_ref[...])
    u = jnp.maximum(u, 0.0)
    u = (u * u).astype(BF16)
    m = _dot(u, w_down_ref[...])
    y_ref[...] = _layer_norm(ALPHA * x1 + m, g2_ref[...], b2_ref[...])


def _layer_call(x, mix_args, ln, w_up, w_down, *, conv, stream_len, tm):
    s = x.shape[0]
    assert s % tm == 0
    assert stream_len & (stream_len - 1) == 0 and (stream_len == 0 or tm % stream_len == 0)
    n = s // tm
    cfg = _LayerCfg(conv=conv, stream_len=stream_len, tm=tm)
    streams = cfg.streams
    row_spec = pl.BlockSpec((tm, D_MODEL), lambda i: (i, 0))
    in_specs = [row_spec]
    args = [x]
    if conv:
        if streams:
            p1, p2, w_in, cw, w_out = mix_args
            in_specs += [row_spec, row_spec]
            args += [p1, p2]
        else:
            w_in, cw, w_out = mix_args
        in_specs += [_const_spec(w_in.shape), _const_spec(cw.shape), _const_spec(w_out.shape)]
        args += [w_in, cw, w_out]
    else:
        a, w_o = mix_args
        in_specs += [pl.BlockSpec((N_PAIRS, tm, PAIR_W), lambda i: (0, i, 0)), _const_spec(w_o.shape)]
        args += [a, w_o]
    in_specs += [_const_spec((1, D_MODEL))] * 4 + [_const_spec(w_up.shape), _const_spec(w_down.shape)]
    args += list(ln) + [w_up, w_down]

    out_shape = [jax.ShapeDtypeStruct((s, D_MODEL), F32)]
    out_specs = [row_spec]
    scratch = []
    if conv:
        if streams:
            out_shape.append(jax.ShapeDtypeStruct((s, D_MODEL), F32))
            out_specs.append(row_spec)
        else:
            out_shape.append(jax.ShapeDtypeStruct((SUBLANES_V7X, D_MODEL), F32))
            out_specs.append(pl.BlockSpec((SUBLANES_V7X, D_MODEL), lambda i: (0, 0)))
            scratch.append(pltpu.VMEM((SUBLANES_V7X, D_MODEL), F32))
    return pl.pallas_call(
        functools.partial(_layer_kernel, cfg=cfg),
        grid=(n,),
        in_specs=in_specs,
        out_specs=out_specs,
        out_shape=out_shape,
        scratch_shapes=scratch,
        compiler_params=pltpu.CompilerParams(
            dimension_semantics=("arbitrary",), vmem_limit_bytes=VMEM_LIMIT_BYTES),
        name="layer_conv" if conv else "layer_attn",
    )(*args)


def _proj_kernel(*refs, scales, tails, tail_rows):
    n = len(scales)
    x_ref = refs[0]
    w_refs = refs[1:1 + n]
    out_refs = list(refs[1 + n:])
    xb = x_ref[...].astype(BF16)
    tm = xb.shape[0]
    for j in range(n):
        pj = _dot(xb, w_refs[j][...])
        if scales[j] != 1.0:
            pj = pj * scales[j]
        _store_pairs(out_refs.pop(0), pj)
        if tails[j]:
            out_refs.pop(0)[...] = pj[tm - tail_rows:, :]


def _proj_call(x, ws, scales, tails, *, tm, keep):
    s = x.shape[0]
    n = s // tm
    tail_rows = min(tm, keep)
    n_tail_blocks = keep // tail_rows
    row_spec = pl.BlockSpec((tm, D_MODEL), lambda i: (i, 0))
    pair_spec = pl.BlockSpec((N_PAIRS, tm, PAIR_W), lambda i: (0, i, 0))
    tail_spec = pl.BlockSpec((tail_rows, D_MODEL),
                             lambda i: (jnp.maximum(i - (n - n_tail_blocks), 0), 0))
    out_shape, out_specs = [], []
    for t in tails:
        out_shape.append(jax.ShapeDtypeStruct((N_PAIRS, s, PAIR_W), BF16))
        out_specs.append(pair_spec)
        if t:
            out_shape.append(jax.ShapeDtypeStruct((keep, D_MODEL), F32))
            out_specs.append(tail_spec)
    return pl.pallas_call(
        functools.partial(_proj_kernel, scales=tuple(scales), tails=tuple(tails), tail_rows=tail_rows),
        grid=(n,),
        in_specs=[row_spec] + [_const_spec(w.shape) for w in ws],
        out_specs=out_specs,
        out_shape=out_shape,
        compiler_params=pltpu.CompilerParams(
            dimension_semantics=("arbitrary",), vmem_limit_bytes=VMEM_LIMIT_BYTES),
        name="proj",
    )(x, *ws)


def _softmax_pv(s, v_parts):
    m = jnp.max(s, axis=-1, keepdims=True)
    e = jnp.exp(s - m)
    l = jnp.sum(e, axis=-1, keepdims=True)
    eb = e.astype(BF16)
    o = None
    off = 0
    for v in v_parts:
        part = _dot(eb[:, off:off + v.shape[0]], v)
        o = part if o is None else o + part
        off += v.shape[0]
    return o / l


def _attn_prompt_kernel(q_ref, kp_ref, kc_ref, vp_ref, vc_ref, bias_ref, o_ref):
    first = jnp.where(pl.program_id(0) == 0, MASK_VALUE, 0.0).astype(F32)
    lane = lax.broadcasted_iota(jnp.int32, (Q_BLOCK, PAIR_W), 1)
    low = lane < HEAD_DIM

    def pair_body(p, carry):
        q2, kp, kc, vp, vc = q_ref[p], kp_ref[p], kc_ref[p], vp_ref[p], vc_ref[p]
        for blk in range(ATT_STEP // Q_BLOCK):
            qb = q2[blk * Q_BLOCK:(blk + 1) * Q_BLOCK]
            past_lo = blk * Q_BLOCK
            cur_hi = (blk + 1) * Q_BLOCK
            k_parts = (kp[past_lo:], kc[:cur_hi])
            v_parts = (vp[past_lo:], vc[:cur_hi])
            outs = []
            for hh in range(2):
                qm = jnp.where(low if hh == 0 else ~low, qb, jnp.zeros_like(qb))
                s_past = _dot_nt(qm, k_parts[0]) + first
                s_cur = _dot_nt(qm, k_parts[1])
                s = jnp.concatenate([s_past, s_cur], axis=-1) + bias_ref[2 * p + hh]
                outs.append(_softmax_pv(s, v_parts))
            o_pair = jnp.where(low, outs[0], outs[1])
            o_ref[p, blk * Q_BLOCK:(blk + 1) * Q_BLOCK, :] = o_pair.astype(BF16)
        return carry

    lax.fori_loop(0, N_PAIRS, pair_body, 0)


def _attn_prompt_call(q, k, v, bias):
    s = q.shape[1]
    n = s // ATT_STEP
    cur = pl.BlockSpec((N_PAIRS, ATT_STEP, PAIR_W), lambda i: (0, i, 0))
    prev = pl.BlockSpec((N_PAIRS, ATT_STEP, PAIR_W), lambda i: (0, jnp.maximum(i - 1, 0), 0))
    return pl.pallas_call(
        _attn_prompt_kernel,
        grid=(n,),
        in_specs=[cur, prev, cur, prev, cur, _const_spec(bias.shape)],
        out_specs=cur,
        out_shape=jax.ShapeDtypeStruct((N_PAIRS, s, PAIR_W), BF16),
        compiler_params=pltpu.CompilerParams(
            dimension_semantics=("arbitrary",), vmem_limit_bytes=VMEM_LIMIT_BYTES),
        name="attn_prompt",
    )(q, k, k, v, v, bias)


def _attn_decode_kernel(q_ref, kn_ref, vn_ref, kc_ref, vc_ref, bias_c_ref, bias_n_ref, o_ref):
    t = q_ref.shape[1]
    lane = lax.broadcasted_iota(jnp.int32, (t, PAIR_W), 1)
    low = lane < HEAD_DIM
    for p in range(N_PAIRS):
        qp = q_ref[p]
        kc = kc_ref[0, :, p * PAIR_W:(p + 1) * PAIR_W].astype(BF16)
        vc = vc_ref[0, :, p * PAIR_W:(p + 1) * PAIR_W].astype(BF16)
        kn, vn = kn_ref[p], vn_ref[p]
        outs = []
        for hh in range(2):
            qm = jnp.where(low if hh == 0 else ~low, qp, jnp.zeros_like(qp))
            s = jnp.concatenate(
                [_dot_nt(qm, kc) + bias_c_ref[2 * p + hh], _dot_nt(qm, kn) + bias_n_ref[2 * p + hh]],
                axis=-1)
            outs.append(_softmax_pv(s, (vc, vn)))
        o_ref[p] = jnp.where(low, outs[0], outs[1]).astype(BF16)


def _attn_decode_call(q, kn, vn, cache_k, cache_v, bias_c, bias_n, *, t, pad_t):
    nb = cache_k.shape[0]
    w = cache_k.shape[1]
    new_spec = pl.BlockSpec((N_PAIRS, pad_t, PAIR_W), lambda i: (0, i, 0))
    q_spec = pl.BlockSpec((N_PAIRS, t, PAIR_W), lambda i: (0, i, 0))
    cache_spec = pl.BlockSpec((1, w, D_MODEL), lambda i: (i, 0, 0))
    return pl.pallas_call(
        _attn_decode_kernel,
        grid=(nb,),
        in_specs=[q_spec, new_spec, new_spec, cache_spec, cache_spec,
                  _const_spec(bias_c.shape), _const_spec(bias_n.shape)],
        out_specs=q_spec,
        out_shape=jax.ShapeDtypeStruct((N_PAIRS, nb * t, PAIR_W), BF16),
        compiler_params=pltpu.CompilerParams(
            dimension_semantics=("arbitrary",), vmem_limit_bytes=VMEM_LIMIT_BYTES),
        name="attn_decode",
    )(q, kn, vn, cache_k, cache_v, bias_c, bias_n)


def _rel_lookup(table, rel):
    idx = jnp.clip(rel, -REL_MAX, REL_MAX) + REL_MAX
    return jnp.transpose(table[idx], (2, 0, 1))


def _prompt_bias(table):
    qi = jnp.arange(Q_BLOCK)[:, None]
    kj = jnp.arange(K_BLOCK)[None, :]
    rel = qi + BAND_PAST - kj
    start = (qi // CHUNK) * CHUNK
    valid = (kj >= start) & (kj < start + BAND_PAST + CHUNK)
    return jnp.where(valid[None], _rel_lookup(table, rel), MASK_VALUE).astype(F32)


def _decode_bias(table, w, t, pad_t):
    qi = jnp.arange(t)[:, None]
    bias_c = _rel_lookup(table, w + qi - jnp.arange(w)[None, :]).astype(F32)
    jn = jnp.arange(pad_t)[None, :]
    bias_n = jnp.where((jn < t)[None], _rel_lookup(table, qi - jn), MASK_VALUE).astype(F32)
    return bias_c, bias_n


def _trunk(x, conv_prev, caches, wts, *, tm, proj_tm, keep):
    s = x.shape[0]
    ln = lambda l: (wts["ln_mix_g"][l][None], wts["ln_mix_b"][l][None],
                    wts["ln_ffn_g"][l][None], wts["ln_ffn_b"][l][None])
    stream_len = 0 if conv_prev is None else s // conv_prev.shape[1]
    ztails = []
    for l in range(N_A):
        mix = (wts["w_in_a"][l], wts["conv_w_a"][l], wts["w_out_a"][l])
        if stream_len:
            p2 = jnp.pad(conv_prev[l], ((0, 0), (0, stream_len - 2), (0, 0))).reshape(s, D_MODEL)
            p1 = jnp.pad(conv_prev[l][:, 1:], ((0, 0), (0, stream_len - 1), (0, 0))).reshape(s, D_MODEL)
            mix = (p1, p2) + mix
        x, zt = _layer_call(x, mix, ln(l), wts["w_up"][l], wts["w_down"][l],
                            conv=True, stream_len=stream_len, tm=tm)
        ztails.append(zt)

    scale = HEAD_DIM ** -0.5
    q, k, kf, v, vf = _proj_call(x, (wts["w_q_b"][0], wts["w_k"], wts["w_v"]),
                                 (scale, 1.0, 1.0), (False, True, True), tm=proj_tm, keep=keep)
    for j in range(DEPTH - N_A):
        l = N_A + j
        if j > 0:
            (q,) = _proj_call(x, (wts["w_q_b"][j],), (scale,), (False,), tm=proj_tm, keep=keep)
        if caches is None:
            a = _attn_prompt_call(q, k, v, _prompt_bias(wts["rel_bias_b"][j]))
        else:
            cache_k, cache_v = caches
            nb, w = cache_k.shape[0], cache_k.shape[1]
            t = s // nb
            pad_t = PAIR_W
            padn = lambda z: jnp.pad(z.reshape(N_PAIRS, nb, t, PAIR_W),
                                     ((0, 0), (0, 0), (0, pad_t - t), (0, 0))
                                     ).reshape(N_PAIRS, nb * pad_t, PAIR_W)
            bias_c, bias_n = _decode_bias(wts["rel_bias_b"][j], w, t, pad_t)
            a = _attn_decode_call(q, padn(k), padn(v), cache_k, cache_v, bias_c, bias_n, t=t, pad_t=pad_t)
        x, = _layer_call(x, (a, wts["w_o_b"][j]), ln(l), wts["w_up"][l], wts["w_down"][l],
                         conv=False, stream_len=0, tm=tm)
    return x, ztails, kf, vf


def kernel(x_prompt, x_sample, cache_conv, cache_k, cache_v, ln_mix_g, ln_mix_b, ln_ffn_g, ln_ffn_b,
           w_up, w_down, w_in_a, conv_w_a, w_out_a, w_k, w_v, w_q_b, w_o_b, rel_bias_b):
    assert x_prompt.shape[0] == 1
    wts = dict(
        ln_mix_g=ln_mix_g, ln_mix_b=ln_mix_b, ln_ffn_g=ln_ffn_g, ln_ffn_b=ln_ffn_b,
        w_up=w_up.astype(BF16), w_down=w_down.astype(BF16), w_in_a=w_in_a.astype(BF16),
        conv_w_a=conv_w_a, w_out_a=w_out_a.astype(BF16), w_k=w_k.astype(BF16), w_v=w_v.astype(BF16),
        w_q_b=w_q_b.astype(BF16), w_o_b=w_o_b.astype(BF16), rel_bias_b=rel_bias_b)

    seq = x_prompt.shape[1]
    keep = min(BAND_PAST, seq)
    yp, ztp, kfp, vfp = _trunk(x_prompt[0], None, None, wts, tm=256, proj_tm=512, keep=keep)
    conv_prompt = jnp.stack([zt[SUBLANES_V7X - 2:] for zt in ztp])[:, None]
    k_prompt = kfp.reshape(1, keep, N_HEADS, HEAD_DIM)
    v_prompt = vfp.reshape(1, keep, N_HEADS, HEAD_DIM)

    nb, t = x_sample.shape[0], x_sample.shape[1]
    w = cache_k.shape[1]
    ns = nb * t
    ys, zts, kfs, vfs = _trunk(
        x_sample.reshape(ns, D_MODEL), cache_conv,
        (cache_k.reshape(nb, w, D_MODEL), cache_v.reshape(nb, w, D_MODEL)),
        wts, tm=ns, proj_tm=ns, keep=ns)
    conv_sample = jnp.stack([zt.reshape(nb, t, D_MODEL)[:, t - 2:] for zt in zts])
    k_sample = kfs.reshape(nb, t, N_HEADS, HEAD_DIM)
    v_sample = vfs.reshape(nb, t, N_HEADS, HEAD_DIM)
    return (yp[None], ys.reshape(nb, t, D_MODEL), conv_prompt, k_prompt, v_prompt,
            conv_sample, k_sample, v_sample)
```

```python
import dataclasses
import functools
import math

import jax
import jax.numpy as jnp
from jax import lax
from jax.experimental import pallas as pl
from jax.experimental.pallas import tpu as pltpu

D_MODEL = 1024
N_HEADS = 16
HEAD_DIM = 64
DEPTH = 4
N_A = 2
CHUNK = 64
N_PREV_CHUNKS = 8
BAND_PAST = N_PREV_CHUNKS * CHUNK
REL_MAX = 128
ALPHA = (2.0 * DEPTH) ** 0.25
LN_EPS = 1e-5
LOG2E = math.log2(math.e)

LANES_V7X = 128
SUBLANES_V7X = 8
VMEM_BYTES_V7X = 64 * 1024 * 1024
VMEM_LIMIT_BYTES = VMEM_BYTES_V7X - 8 * 1024 * 1024

PAIR_W = LANES_V7X
N_PAIRS = D_MODEL // PAIR_W
Q_BLOCK = 4 * CHUNK
K_BLOCK = BAND_PAST + Q_BLOCK
ATT_STEP = 2 * Q_BLOCK
PAIRS_PER_ITER = 8
REL_RING = 1024
MASK_VALUE = -1e30

BF16 = jnp.bfloat16
F32 = jnp.float32


def _const_spec(shape):
    zeros = (0,) * len(shape)
    return pl.BlockSpec(shape, lambda i: zeros, pipeline_mode=pl.Buffered(1))


def _params():
    return pltpu.CompilerParams(dimension_semantics=("arbitrary",), vmem_limit_bytes=VMEM_LIMIT_BYTES)


def _dot(a, b):
    return jnp.dot(a, b, preferred_element_type=F32)


def _dot_nt(a, b):
    return lax.dot_general(a, b, (((1,), (1,)), ((), ())), preferred_element_type=F32)


def _layer_norm(x, g, b):
    mu = jnp.mean(x, axis=-1, keepdims=True)
    xc = x - mu
    var = jnp.mean(xc * xc, axis=-1, keepdims=True)
    return xc * lax.rsqrt(var + LN_EPS) * g + b


def _pairs_to_rows(ref):
    return jnp.concatenate([ref[p] for p in range(N_PAIRS)], axis=-1)


def _store_pairs(ref, val):
    for p in range(N_PAIRS):
        ref[p] = val[:, p * PAIR_W:(p + 1) * PAIR_W].astype(BF16)


@dataclasses.dataclass(frozen=True)
class _LayerCfg:
    conv: bool
    stream_len: int
    pairs: bool
    tm: int

    @property
    def streams(self):
        return self.stream_len > 0


def _layer_kernel(*refs, cfg):
    it = iter(refs)
    x_ref = next(it)
    if cfg.conv:
        if cfg.streams:
            p1_ref, p2_ref = next(it), next(it)
        w_in_ref, cw_ref, w_mix_ref = next(it), next(it), next(it)
    else:
        a_ref, w_mix_ref = next(it), next(it)
    g1_ref, b1_ref, g2_ref, b2_ref = next(it), next(it), next(it), next(it)
    w_up_ref, w_down_ref = next(it), next(it)
    y_ref = next(it)
    if cfg.conv:
        ztail_ref = next(it)
        if not cfg.streams:
            carry_ref = next(it)

    tm = cfg.tm
    x = x_ref[...]
    if cfg.conv:
        bch = _dot(x.astype(BF16), w_in_ref[...])
        b_gate = bch[:, :D_MODEL]
        z = bch[:, D_MODEL:2 * D_MODEL] * bch[:, 2 * D_MODEL:]
        r1 = pltpu.roll(z, 1, 0)
        r2 = pltpu.roll(z, 2, 0)
        row = lax.broadcasted_iota(jnp.int32, (tm, D_MODEL), 0)
        if cfg.streams:
            rm = row & (cfg.stream_len - 1)
            zm1 = jnp.where(rm == 0, p1_ref[...], r1)
            zm2 = jnp.where(rm < 2, p2_ref[...], r2)
            ztail_ref[...] = z
        else:
            @pl.when(pl.program_id(0) == 0)
            def _():
                carry_ref[...] = jnp.zeros_like(carry_ref)
            prev = carry_ref[...]
            pm1 = prev[SUBLANES_V7X - 1:SUBLANES_V7X, :]
            pm2 = prev[SUBLANES_V7X - 2:SUBLANES_V7X - 1, :]
            zm1 = jnp.where(row == 0, pm1, r1)
            zm2 = jnp.where(row == 0, pm2, jnp.where(row == 1, pm1, r2))
            tail = z[tm - SUBLANES_V7X:, :]
            carry_ref[...] = tail
            ztail_ref[...] = tail
        cw = cw_ref[...]
        yc = cw[0:1, :] * zm2 + cw[1:2, :] * zm1 + cw[2:3, :] * z
        mix_in = (b_gate * yc).astype(BF16)
    else:
        mix_in = _pairs_to_rows(a_ref) if cfg.pairs else a_ref[...]
    h = _dot(mix_in, w_mix_ref[...])
    x1 = _layer_norm(ALPHA * x + h, g1_ref[...], b1_ref[...])
    u = _dot(x1.astype(BF16), w_up_ref[...])
    u = jnp.maximum(u, 0.0)
    u = (u * u).astype(BF16)
    m = _dot(u, w_down_ref[...])
    y_ref[...] = _layer_norm(ALPHA * x1 + m, g2_ref[...], b2_ref[...])


def _layer_call(x, mix_args, ln, w_up, w_down, *, conv, stream_len, pairs, tm):
    s = x.shape[0]
    assert s % tm == 0
    assert stream_len & (stream_len - 1) == 0 and (stream_len == 0 or tm % stream_len == 0)
    n = s // tm
    cfg = _LayerCfg(conv=conv, stream_len=stream_len, pairs=pairs, tm=tm)
    streams = cfg.streams
    row_spec = pl.BlockSpec((tm, D_MODEL), lambda i: (i, 0))
    pair_spec = pl.BlockSpec((N_PAIRS, tm, PAIR_W), lambda i: (0, i, 0))
    in_specs = [row_spec]
    args = [x]
    if conv:
        if streams:
            p1, p2, w_in, cw, w_out = mix_args
            in_specs += [row_spec, row_spec]
            args += [p1, p2]
        else:
            w_in, cw, w_out = mix_args
        in_specs += [_const_spec(w_in.shape), _const_spec(cw.shape), _const_spec(w_out.shape)]
        args += [w_in, cw, w_out]
    else:
        a, w_o = mix_args
        in_specs += [pair_spec if pairs else row_spec, _const_spec(w_o.shape)]
        args += [a, w_o]
    in_specs += [_const_spec((1, D_MODEL))] * 4 + [_const_spec(w_up.shape), _const_spec(w_down.shape)]
    args += list(ln) + [w_up, w_down]

    out_shape = [jax.ShapeDtypeStruct((s, D_MODEL), F32)]
    out_specs = [row_spec]
    scratch = []
    if conv:
        if streams:
            out_shape.append(jax.ShapeDtypeStruct((s, D_MODEL), F32))
            out_specs.append(row_spec)
        else:
            out_shape.append(jax.ShapeDtypeStruct((SUBLANES_V7X, D_MODEL), F32))
            out_specs.append(pl.BlockSpec((SUBLANES_V7X, D_MODEL), lambda i: (0, 0)))
            scratch.append(pltpu.VMEM((SUBLANES_V7X, D_MODEL), F32))
    return pl.pallas_call(
        functools.partial(_layer_kernel, cfg=cfg),
        grid=(n,),
        in_specs=in_specs,
        out_specs=out_specs,
        out_shape=out_shape,
        scratch_shapes=scratch,
        compiler_params=_params(),
        name="layer_conv" if conv else "layer_attn",
    )(*args)


def _proj_kernel(*refs, scales, tails, tail_rows, pairs):
    n = len(scales)
    x_ref = refs[0]
    w_refs = refs[1:1 + n]
    out_refs = list(refs[1 + n:])
    xb = x_ref[...].astype(BF16)
    tm = xb.shape[0]
    for j in range(n):
        pj = _dot(xb, w_refs[j][...])
        if scales[j] != 1.0:
            pj = pj * scales[j]
        if pairs:
            _store_pairs(out_refs.pop(0), pj)
        else:
            out_refs.pop(0)[...] = pj.astype(BF16)
        if tails[j]:
            out_refs.pop(0)[...] = pj[tm - tail_rows:, :]


def _proj_call(x, ws, scales, tails, *, tm, keep, pairs):
    s = x.shape[0]
    n = s // tm
    tail_rows = min(tm, keep)
    n_tail_blocks = keep // tail_rows
    row_spec = pl.BlockSpec((tm, D_MODEL), lambda i: (i, 0))
    pair_spec = pl.BlockSpec((N_PAIRS, tm, PAIR_W), lambda i: (0, i, 0))
    tail_spec = pl.BlockSpec((tail_rows, D_MODEL),
                             lambda i: (jnp.maximum(i - (n - n_tail_blocks), 0), 0))
    out_shape, out_specs = [], []
    for t in tails:
        if pairs:
            out_shape.append(jax.ShapeDtypeStruct((N_PAIRS, s, PAIR_W), BF16))
            out_specs.append(pair_spec)
        else:
            out_shape.append(jax.ShapeDtypeStruct((s, D_MODEL), BF16))
            out_specs.append(row_spec)
        if t:
            out_shape.append(jax.ShapeDtypeStruct((keep, D_MODEL), F32))
            out_specs.append(tail_spec)
    return pl.pallas_call(
        functools.partial(_proj_kernel, scales=tuple(scales), tails=tuple(tails),
                          tail_rows=tail_rows, pairs=pairs),
        grid=(n,),
        in_specs=[row_spec] + [_const_spec(w.shape) for w in ws],
        out_specs=out_specs,
        out_shape=out_shape,
        compiler_params=_params(),
        name="proj",
    )(x, *ws)


def _rel_bias_rows(rvec_ref, h, rows, valid):
    row = jnp.broadcast_to(rvec_ref[h:h + 1, :], (rows, REL_RING))
    t = pltpu.roll(row, 0, 1, stride=1, stride_axis=0)
    return jnp.where(valid, t * LOG2E, MASK_VALUE)


def _softmax_pv(s, v_parts):
    m = jnp.max(s, axis=-1, keepdims=True)
    e = jnp.exp2(s - m)
    l = jnp.sum(e, axis=-1, keepdims=True)
    eb = e.astype(BF16)
    o = None
    off = 0
    for v in v_parts:
        part = _dot(eb[:, off:off + v.shape[0]], v)
        o = part if o is None else o + part
        off += v.shape[0]
    return o / l


def _attn_prompt_kernel(q_ref, kp_ref, kc_ref, vp_ref, vc_ref, rvec_ref, o_ref, bias_ref):
    step = pl.program_id(0)
    low = lax.broadcasted_iota(jnp.int32, (Q_BLOCK, PAIR_W), 1) < HEAD_DIM

    def build_bias(first_key):
        qi = lax.broadcasted_iota(jnp.int32, (Q_BLOCK, REL_RING), 0)
        kj = lax.broadcasted_iota(jnp.int32, (Q_BLOCK, REL_RING), 1)
        start = (qi >> 6) << 6
        valid = (kj >= start) & (kj < start + (BAND_PAST + CHUNK)) & (kj >= first_key)
        for h in range(N_HEADS):
            bias_ref[h * Q_BLOCK:(h + 1) * Q_BLOCK, :] = (
                _rel_bias_rows(rvec_ref, h, Q_BLOCK, valid)[:, :K_BLOCK])

    for blk in range(ATT_STEP // Q_BLOCK):
        past_lo = blk * Q_BLOCK
        cur_hi = (blk + 1) * Q_BLOCK
        rebuild = (step == 0) if blk > 0 else (step <= 1)
        pl.when(rebuild)(functools.partial(
            build_bias, jnp.where(step == 0, BAND_PAST - past_lo, 0)))

        def one_pair(p):
            qb = q_ref[p, blk * Q_BLOCK:(blk + 1) * Q_BLOCK, :]
            qs = jnp.concatenate([jnp.where(low, qb, jnp.zeros_like(qb)),
                                  jnp.where(low, jnp.zeros_like(qb), qb)], axis=0)
            s = jnp.concatenate([_dot_nt(qs, kp_ref[p, past_lo:, :]),
                                 _dot_nt(qs, kc_ref[p, :cur_hi, :])], axis=-1)
            row0 = pl.multiple_of(p * (2 * Q_BLOCK), 2 * Q_BLOCK)
            s = s + bias_ref[pl.ds(row0, 2 * Q_BLOCK), :]
            o = _softmax_pv(s, (vp_ref[p, past_lo:, :], vc_ref[p, :cur_hi, :]))
            o_pair = jnp.where(low, o[:Q_BLOCK], o[Q_BLOCK:])
            o_ref[p, blk * Q_BLOCK:(blk + 1) * Q_BLOCK, :] = o_pair.astype(BF16)

        def pairs_body(it, carry):
            for u in range(PAIRS_PER_ITER):
                one_pair(it * PAIRS_PER_ITER + u)
            return carry

        lax.fori_loop(0, N_PAIRS // PAIRS_PER_ITER, pairs_body, 0)


def _attn_prompt_call(q, k, v, rvec):
    s = q.shape[1]
    assert s % ATT_STEP == 0 and ATT_STEP >= BAND_PAST
    n = s // ATT_STEP
    cur = pl.BlockSpec((N_PAIRS, ATT_STEP, PAIR_W), lambda i: (0, i, 0))
    prev = pl.BlockSpec((N_PAIRS, ATT_STEP, PAIR_W), lambda i: (0, jnp.maximum(i - 1, 0), 0))
    return pl.pallas_call(
        _attn_prompt_kernel,
        grid=(n,),
        in_specs=[cur, prev, cur, prev, cur, _const_spec(rvec.shape)],
        out_specs=cur,
        out_shape=jax.ShapeDtypeStruct((N_PAIRS, s, PAIR_W), BF16),
        scratch_shapes=[pltpu.VMEM((N_HEADS * Q_BLOCK, K_BLOCK), F32)],
        compiler_params=_params(),
        name="attn_prompt",
    )(q, k, k, v, v, rvec)


def _attn_decode_kernel(q_ref, kn_ref, vn_ref, kc_ref, vc_ref, rvec_ref, o_ref, bias_ref):
    t = q_ref.shape[0]
    w = kc_ref.shape[1]
    kpad = bias_ref.shape[1]

    @pl.when(pl.program_id(0) == 0)
    def _():
        kj = lax.broadcasted_iota(jnp.int32, (t, REL_RING), 1)
        valid = kj < w + t
        for h in range(N_HEADS):
            bias_ref[h * t:(h + 1) * t, :] = _rel_bias_rows(rvec_ref, h, t, valid)[:, :kpad]

    q = q_ref[...]
    head_of_lane = lax.broadcasted_iota(jnp.int32, (t, D_MODEL), 1) >> 6
    qs = jnp.concatenate(
        [jnp.where(head_of_lane == h, q, jnp.zeros_like(q)) for h in range(N_HEADS)], axis=0)
    zeros = jnp.zeros((kpad - w - t, D_MODEL), BF16)
    kk = jnp.concatenate([kc_ref[0].astype(BF16), kn_ref[...], zeros], axis=0)
    vv = jnp.concatenate([vc_ref[0].astype(BF16), vn_ref[...], zeros], axis=0)
    o = _softmax_pv(_dot_nt(qs, kk) + bias_ref[...], (vv,))
    out = jnp.zeros((t, D_MODEL), F32)
    for h in range(N_HEADS):
        out = out + jnp.where(head_of_lane == h, o[h * t:(h + 1) * t, :], 0.0)
    o_ref[...] = out.astype(BF16)


def _attn_decode_call(q, kn, vn, cache_k, cache_v, rvec, *, t):
    nb, w = cache_k.shape[0], cache_k.shape[1]
    assert w == BAND_PAST
    kpad = -(-(w + t) // LANES_V7X) * LANES_V7X
    assert kpad + t <= REL_RING
    new_spec = pl.BlockSpec((t, D_MODEL), lambda i: (i, 0))
    cache_spec = pl.BlockSpec((1, w, D_MODEL), lambda i: (i, 0, 0))
    return pl.pallas_call(
        _attn_decode_kernel,
        grid=(nb,),
        in_specs=[new_spec, new_spec, new_spec, cache_spec, cache_spec, _const_spec(rvec.shape)],
        out_specs=new_spec,
        out_shape=jax.ShapeDtypeStruct((nb * t, D_MODEL), BF16),
        scratch_shapes=[pltpu.VMEM((N_HEADS * t, kpad), F32)],
        compiler_params=_params(),
        name="attn_decode",
    )(q, kn, vn, cache_k, cache_v, rvec)


def _rel_vector(table):
    m = jnp.arange(REL_RING)
    d = jnp.where(m < K_BLOCK, m, m - REL_RING)
    idx = jnp.clip(BAND_PAST - d, -REL_MAX, REL_MAX) + REL_MAX
    return table[idx].T.astype(F32)


def _trunk(x, conv_prev, caches, wts, *, tm, proj_tm, keep):
    s = x.shape[0]
    ln = lambda l: (wts["ln_mix_g"][l][None], wts["ln_mix_b"][l][None],
                    wts["ln_ffn_g"][l][None], wts["ln_ffn_b"][l][None])
    stream_len = 0 if conv_prev is None else s // conv_prev.shape[1]
    pairs = caches is None
    ztails = []
    for l in range(N_A):
        mix = (wts["w_in_a"][l], wts["conv_w_a"][l], wts["w_out_a"][l])
        if stream_len:
            p2 = jnp.pad(conv_prev[l], ((0, 0), (0, stream_len - 2), (0, 0))).reshape(s, D_MODEL)
            p1 = jnp.pad(conv_prev[l][:, 1:], ((0, 0), (0, stream_len - 1), (0, 0))).reshape(s, D_MODEL)
            mix = (p1, p2) + mix
        x, zt = _layer_call(x, mix, ln(l), wts["w_up"][l], wts["w_down"][l],
                            conv=True, stream_len=stream_len, pairs=False, tm=tm)
        ztails.append(zt)

    q_scale = HEAD_DIM ** -0.5 * LOG2E
    q, k, kf, v, vf = _proj_call(x, (wts["w_q_b"][0], wts["w_k"], wts["w_v"]),
                                 (q_scale, 1.0, 1.0), (False, True, True),
                                 tm=proj_tm, keep=keep, pairs=pairs)
    for j in range(DEPTH - N_A):
        l = N_A + j
        if j > 0:
            (q,) = _proj_call(x, (wts["w_q_b"][j],), (q_scale,), (False,),
                              tm=proj_tm, keep=keep, pairs=pairs)
        rvec = _rel_vector(wts["rel_bias_b"][j])
        if caches is None:
            a = _attn_prompt_call(q, k, v, rvec)
        else:
            a = _attn_decode_call(q, k, v, caches[0], caches[1], rvec, t=s // caches[0].shape[0])
        x, = _layer_call(x, (a, wts["w_o_b"][j]), ln(l), wts["w_up"][l], wts["w_down"][l],
                         conv=False, stream_len=0, pairs=pairs, tm=tm)
    return x, ztails, kf, vf


def kernel(x_prompt, x_sample, cache_conv, cache_k, cache_v, ln_mix_g, ln_mix_b, ln_ffn_g, ln_ffn_b,
           w_up, w_down, w_in_a, conv_w_a, w_out_a, w_k, w_v, w_q_b, w_o_b, rel_bias_b):
    assert x_prompt.shape[0] == 1
    wts = dict(
        ln_mix_g=ln_mix_g, ln_mix_b=ln_mix_b, ln_ffn_g=ln_ffn_g, ln_ffn_b=ln_ffn_b,
        w_up=w_up.astype(BF16), w_down=w_down.astype(BF16), w_in_a=w_in_a.astype(BF16),
        conv_w_a=conv_w_a, w_out_a=w_out_a.astype(BF16), w_k=w_k.astype(BF16), w_v=w_v.astype(BF16),
        w_q_b=w_q_b.astype(BF16), w_o_b=w_o_b.astype(BF16), rel_bias_b=rel_bias_b)

    seq = x_prompt.shape[1]
    keep = min(BAND_PAST, seq)
    yp, ztp, kfp, vfp = _trunk(x_prompt[0], None, None, wts, tm=256, proj_tm=512, keep=keep)
    conv_prompt = jnp.stack([zt[SUBLANES_V7X - 2:] for zt in ztp])[:, None]
    k_prompt = kfp.reshape(1, keep, N_HEADS, HEAD_DIM)
    v_prompt = vfp.reshape(1, keep, N_HEADS, HEAD_DIM)

    nb, t = x_sample.shape[0], x_sample.shape[1]
    w = cache_k.shape[1]
    ns = nb * t
    ys, zts, kfs, vfs = _trunk(
        x_sample.reshape(ns, D_MODEL), cache_conv,
        (cache_k.reshape(nb, w, D_MODEL), cache_v.reshape(nb, w, D_MODEL)),
        wts, tm=ns, proj_tm=ns, keep=ns)
    conv_sample = jnp.stack([zt.reshape(nb, t, D_MODEL)[:, t - 2:] for zt in zts])
    k_sample = kfs.reshape(nb, t, N_HEADS, HEAD_DIM)
    v_sample = vfs.reshape(nb, t, N_HEADS, HEAD_DIM)
    return (yp[None], ys.reshape(nb, t, D_MODEL), conv_prompt, k_prompt, v_prompt,
            conv_sample, k_sample, v_sample)
```

```python
import dataclasses
import functools
import math

import jax
import jax.numpy as jnp
from jax import lax
from jax.experimental import pallas as pl
from jax.experimental.pallas import tpu as pltpu

D_MODEL = 1024
N_HEADS = 16
HEAD_DIM = 64
DEPTH = 4
N_A = 2
CHUNK = 64
N_PREV_CHUNKS = 8
BAND_PAST = N_PREV_CHUNKS * CHUNK
REL_MAX = 128
ALPHA = (2.0 * DEPTH) ** 0.25
LN_EPS = 1e-5
LOG2E = math.log2(math.e)

LANES_V7X = 128
SUBLANES_V7X = 8
VMEM_BYTES_V7X = 64 * 1024 * 1024
VMEM_LIMIT_BYTES = VMEM_BYTES_V7X - 8 * 1024 * 1024

PAIR_W = LANES_V7X
N_PAIRS = D_MODEL // PAIR_W
Q_BLOCK = 4 * CHUNK
K_BLOCK = BAND_PAST + Q_BLOCK
ATT_STEP = 2 * Q_BLOCK
REL_RING = 1024
MASK_VALUE = -1e30

BF16 = jnp.bfloat16
F32 = jnp.float32


def _const_spec(shape):
    zeros = (0,) * len(shape)
    return pl.BlockSpec(shape, lambda i: zeros, pipeline_mode=pl.Buffered(1))


def _params():
    return pltpu.CompilerParams(dimension_semantics=("arbitrary",), vmem_limit_bytes=VMEM_LIMIT_BYTES)


def _dot(a, b):
    return jnp.dot(a, b, preferred_element_type=F32)


def _dot_nt(a, b):
    return lax.dot_general(a, b, (((1,), (1,)), ((), ())), preferred_element_type=F32)


def _layer_norm(x, g, b):
    mu = jnp.mean(x, axis=-1, keepdims=True)
    xc = x - mu
    var = jnp.mean(xc * xc, axis=-1, keepdims=True)
    return xc * lax.rsqrt(var + LN_EPS) * g + b


def _store_pairs(ref, val):
    for p in range(N_PAIRS):
        ref[p] = val[:, p * PAIR_W:(p + 1) * PAIR_W].astype(BF16)


@dataclasses.dataclass(frozen=True)
class _LayerCfg:
    conv: bool
    stream_len: int
    pairs: bool
    tm: int
    n_sub: int

    @property
    def streams(self):
        return self.stream_len > 0


def _layer_kernel(*refs, cfg):
    it = iter(refs)
    x_ref = next(it)
    if cfg.conv:
        if cfg.streams:
            p1_ref, p2_ref = next(it), next(it)
        w_in_ref, cw_ref, w_mix_ref = next(it), next(it), next(it)
    else:
        a_ref, w_mix_ref = next(it), next(it)
    g1_ref, b1_ref, g2_ref, b2_ref = next(it), next(it), next(it), next(it)
    w_up_ref, w_down_ref = next(it), next(it)
    y_ref = next(it)
    if cfg.conv:
        ztail_ref = next(it)
        if not cfg.streams:
            carry_ref = next(it)

    if cfg.conv and not cfg.streams:
        @pl.when(pl.program_id(0) == 0)
        def _():
            carry_ref[...] = jnp.zeros_like(carry_ref)
        prev = carry_ref[...]

    sm = cfg.tm // cfg.n_sub
    subs = range(cfg.n_sub)
    rows = [slice(sub * sm, (sub + 1) * sm) for sub in subs]
    x = [x_ref[r, :] for r in rows]
    if cfg.conv:
        bch = [_dot(x[i].astype(BF16), w_in_ref[...]) for i in subs]
        row = lax.broadcasted_iota(jnp.int32, (sm, D_MODEL), 0)
        cw = cw_ref[...]
        mix_in = []
        for i in subs:
            b_gate = bch[i][:, :D_MODEL]
            z = bch[i][:, D_MODEL:2 * D_MODEL] * bch[i][:, 2 * D_MODEL:]
            r1 = pltpu.roll(z, 1, 0)
            r2 = pltpu.roll(z, 2, 0)
            if cfg.streams:
                rm = row & (cfg.stream_len - 1)
                zm1 = jnp.where(rm == 0, p1_ref[rows[i], :], r1)
                zm2 = jnp.where(rm < 2, p2_ref[rows[i], :], r2)
                ztail_ref[rows[i], :] = z
            else:
                pm1 = prev[SUBLANES_V7X - 1:SUBLANES_V7X, :]
                pm2 = prev[SUBLANES_V7X - 2:SUBLANES_V7X - 1, :]
                zm1 = jnp.where(row == 0, pm1, r1)
                zm2 = jnp.where(row == 0, pm2, jnp.where(row == 1, pm1, r2))
                prev = z[sm - SUBLANES_V7X:, :]
            yc = cw[0:1, :] * zm2 + cw[1:2, :] * zm1 + cw[2:3, :] * z
            mix_in.append((b_gate * yc).astype(BF16))
        if not cfg.streams:
            carry_ref[...] = prev
            ztail_ref[...] = prev
    elif cfg.pairs:
        mix_in = [jnp.concatenate([a_ref[p, r, :] for p in range(N_PAIRS)], axis=-1) for r in rows]
    else:
        mix_in = [a_ref[r, :] for r in rows]
    h = [_dot(mix_in[i], w_mix_ref[...]) for i in subs]
    x1 = [_layer_norm(ALPHA * x[i] + h[i], g1_ref[...], b1_ref[...]) for i in subs]
    u = [_dot(x1[i].astype(BF16), w_up_ref[...]) for i in subs]
    u = [jnp.maximum(u[i], 0.0) for i in subs]
    m = [_dot((u[i] * u[i]).astype(BF16), w_down_ref[...]) for i in subs]
    for i in subs:
        y_ref[rows[i], :] = _layer_norm(ALPHA * x1[i] + m[i], g2_ref[...], b2_ref[...])


def _layer_call(x, mix_args, ln, w_up, w_down, *, conv, stream_len, pairs, tm, n_sub):
    s = x.shape[0]
    assert s % tm == 0 and tm % n_sub == 0
    assert stream_len & (stream_len - 1) == 0 and (stream_len == 0 or (tm // n_sub) % stream_len == 0)
    n = s // tm
    cfg = _LayerCfg(conv=conv, stream_len=stream_len, pairs=pairs, tm=tm, n_sub=n_sub)
    streams = cfg.streams
    row_spec = pl.BlockSpec((tm, D_MODEL), lambda i: (i, 0))
    pair_spec = pl.BlockSpec((N_PAIRS, tm, PAIR_W), lambda i: (0, i, 0))
    in_specs = [row_spec]
    args = [x]
    if conv:
        if streams:
            p1, p2, w_in, cw, w_out = mix_args
            in_specs += [row_spec, row_spec]
            args += [p1, p2]
        else:
            w_in, cw, w_out = mix_args
        in_specs += [_const_spec(w_in.shape), _const_spec(cw.shape), _const_spec(w_out.shape)]
        args += [w_in, cw, w_out]
    else:
        a, w_o = mix_args
        in_specs += [pair_spec if pairs else row_spec, _const_spec(w_o.shape)]
        args += [a, w_o]
    in_specs += [_const_spec((1, D_MODEL))] * 4 + [_const_spec(w_up.shape), _const_spec(w_down.shape)]
    args += list(ln) + [w_up, w_down]

    out_shape = [jax.ShapeDtypeStruct((s, D_MODEL), F32)]
    out_specs = [row_spec]
    scratch = []
    if conv:
        if streams:
            out_shape.append(jax.ShapeDtypeStruct((s, D_MODEL), F32))
            out_specs.append(row_spec)
        else:
            out_shape.append(jax.ShapeDtypeStruct((SUBLANES_V7X, D_MODEL), F32))
            out_specs.append(pl.BlockSpec((SUBLANES_V7X, D_MODEL), lambda i: (0, 0)))
            scratch.append(pltpu.VMEM((SUBLANES_V7X, D_MODEL), F32))
    return pl.pallas_call(
        functools.partial(_layer_kernel, cfg=cfg),
        grid=(n,),
        in_specs=in_specs,
        out_specs=out_specs,
        out_shape=out_shape,
        scratch_shapes=scratch,
        compiler_params=_params(),
        name="layer_conv" if conv else "layer_attn",
    )(*args)


def _proj_kernel(*refs, scales, tails, tail_rows, first_tail_step, pairs):
    n = len(scales)
    x_ref = refs[0]
    w_refs = refs[1:1 + n]
    out_refs = list(refs[1 + n:])
    xb = x_ref[...].astype(BF16)
    tm = xb.shape[0]
    for j in range(n):
        pj = _dot(xb, w_refs[j][...])
        if scales[j] != 1.0:
            pj = pj * scales[j]
        if pairs:
            _store_pairs(out_refs.pop(0), pj)
        else:
            out_refs.pop(0)[...] = pj.astype(BF16)
        if tails[j]:
            tail_ref = out_refs.pop(0)

            @pl.when(pl.program_id(0) >= first_tail_step)
            def _(tail_ref=tail_ref, pj=pj):
                for h in range(N_HEADS):
                    tail_ref[:, h, :] = pj[tm - tail_rows:, h * HEAD_DIM:(h + 1) * HEAD_DIM]


def _proj_call(x, ws, scales, tails, *, tm, keep, pairs):
    s = x.shape[0]
    assert s % tm == 0
    n = s // tm
    tail_rows = min(tm, keep)
    assert keep % tail_rows == 0
    n_tail_blocks = keep // tail_rows
    row_spec = pl.BlockSpec((tm, D_MODEL), lambda i: (i, 0))
    pair_spec = pl.BlockSpec((N_PAIRS, tm, PAIR_W), lambda i: (0, i, 0))
    first_tail_step = n - n_tail_blocks
    tail_spec = pl.BlockSpec((tail_rows, N_HEADS, HEAD_DIM),
                             lambda i: (jnp.maximum(i - first_tail_step, 0), 0, 0))
    out_shape, out_specs = [], []
    for t in tails:
        if pairs:
            out_shape.append(jax.ShapeDtypeStruct((N_PAIRS, s, PAIR_W), BF16))
            out_specs.append(pair_spec)
        else:
            out_shape.append(jax.ShapeDtypeStruct((s, D_MODEL), BF16))
            out_specs.append(row_spec)
        if t:
            out_shape.append(jax.ShapeDtypeStruct((keep, N_HEADS, HEAD_DIM), F32))
            out_specs.append(tail_spec)
    return pl.pallas_call(
        functools.partial(_proj_kernel, scales=tuple(scales), tails=tuple(tails),
                          tail_rows=tail_rows, first_tail_step=first_tail_step, pairs=pairs),
        grid=(n,),
        in_specs=[row_spec] + [_const_spec(w.shape) for w in ws],
        out_specs=out_specs,
        out_shape=out_shape,
        compiler_params=_params(),
        name="proj",
    )(x, *ws)


def _rel_bias_rows(rvec_ref, h, rows, valid):
    row = jnp.broadcast_to(rvec_ref[h:h + 1, :], (rows, REL_RING))
    t = pltpu.roll(row, 0, 1, stride=1, stride_axis=0)
    return jnp.where(valid, t * LOG2E, MASK_VALUE)


def _softmax_pv(s, v_parts):
    m = jnp.max(s, axis=-1, keepdims=True)
    e = jnp.exp2(s - m)
    l = jnp.sum(e, axis=-1, keepdims=True)
    eb = e.astype(BF16)
    o = None
    off = 0
    for v in v_parts:
        part = _dot(eb[:, off:off + v.shape[0]], v)
        o = part if o is None else o + part
        off += v.shape[0]
    return o / l


def _attn_prompt_kernel(q_ref, kp_ref, kc_ref, vp_ref, vc_ref, rvec_ref, o_ref, bias_ref):
    step = pl.program_id(0)
    low = lax.broadcasted_iota(jnp.int32, (Q_BLOCK, PAIR_W), 1) < HEAD_DIM

    def build_bias(first_key):
        qi = lax.broadcasted_iota(jnp.int32, (Q_BLOCK, REL_RING), 0)
        kj = lax.broadcasted_iota(jnp.int32, (Q_BLOCK, REL_RING), 1)
        start = (qi >> 6) << 6
        valid = (kj >= start) & (kj < start + (BAND_PAST + CHUNK)) & (kj >= first_key)
        for h in range(N_HEADS):
            bias_ref[h * Q_BLOCK:(h + 1) * Q_BLOCK, :] = (
                _rel_bias_rows(rvec_ref, h, Q_BLOCK, valid)[:, :K_BLOCK])

    for blk in range(ATT_STEP // Q_BLOCK):
        past_lo = blk * Q_BLOCK
        cur_hi = (blk + 1) * Q_BLOCK
        rebuild = (step == 0) if blk > 0 else (step <= 1)
        pl.when(rebuild)(functools.partial(
            build_bias, jnp.where(step == 0, BAND_PAST - past_lo, 0)))

        q_rows = slice(blk * Q_BLOCK, (blk + 1) * Q_BLOCK)

        def scores(p):
            qb = q_ref[p, q_rows, :]
            qs = jnp.concatenate([jnp.where(low, qb, jnp.zeros_like(qb)),
                                  jnp.where(low, jnp.zeros_like(qb), qb)], axis=0)
            s = jnp.concatenate([_dot_nt(qs, kp_ref[p, past_lo:, :]),
                                 _dot_nt(qs, kc_ref[p, :cur_hi, :])], axis=-1)
            return s + bias_ref[2 * p * Q_BLOCK:2 * (p + 1) * Q_BLOCK, :]

        def finish(p, s):
            o = _softmax_pv(s, (vp_ref[p, past_lo:, :], vc_ref[p, :cur_hi, :]))
            o_ref[p, q_rows, :] = jnp.where(low, o[:Q_BLOCK], o[Q_BLOCK:]).astype(BF16)

        s_next = scores(0)
        for p in range(N_PAIRS):
            s_cur = s_next
            if p + 1 < N_PAIRS:
                s_next = scores(p + 1)
            finish(p, s_cur)


def _attn_prompt_call(q, k, v, rvec):
    s = q.shape[1]
    assert s % ATT_STEP == 0 and ATT_STEP >= BAND_PAST
    n = s // ATT_STEP
    cur = pl.BlockSpec((N_PAIRS, ATT_STEP, PAIR_W), lambda i: (0, i, 0))
    prev = pl.BlockSpec((N_PAIRS, ATT_STEP, PAIR_W), lambda i: (0, jnp.maximum(i - 1, 0), 0))
    return pl.pallas_call(
        _attn_prompt_kernel,
        grid=(n,),
        in_specs=[cur, prev, cur, prev, cur, _const_spec(rvec.shape)],
        out_specs=cur,
        out_shape=jax.ShapeDtypeStruct((N_PAIRS, s, PAIR_W), BF16),
        scratch_shapes=[pltpu.VMEM((N_HEADS * Q_BLOCK, K_BLOCK), F32)],
        compiler_params=_params(),
        name="attn_prompt",
    )(q, k, k, v, v, rvec)


def _attn_decode_kernel(q_ref, kn_ref, vn_ref, kc_ref, vc_ref, rvec_ref, o_ref, bias_ref):
    t = q_ref.shape[0]
    w = kc_ref.shape[1]
    kpad = bias_ref.shape[1]

    @pl.when(pl.program_id(0) == 0)
    def _():
        kj = lax.broadcasted_iota(jnp.int32, (t, REL_RING), 1)
        valid = kj < w + t
        for h in range(N_HEADS):
            bias_ref[h * t:(h + 1) * t, :] = _rel_bias_rows(rvec_ref, h, t, valid)[:, :kpad]

    q = q_ref[...]
    head_of_lane = lax.broadcasted_iota(jnp.int32, (t, D_MODEL), 1) >> 6
    qs = jnp.concatenate(
        [jnp.where(head_of_lane == h, q, jnp.zeros_like(q)) for h in range(N_HEADS)], axis=0)
    zeros = jnp.zeros((kpad - w - t, D_MODEL), BF16)
    kk = jnp.concatenate([kc_ref[0].astype(BF16), kn_ref[...], zeros], axis=0)
    vv = jnp.concatenate([vc_ref[0].astype(BF16), vn_ref[...], zeros], axis=0)
    o = _softmax_pv(_dot_nt(qs, kk) + bias_ref[...], (vv,))
    out = jnp.zeros((t, D_MODEL), F32)
    for h in range(N_HEADS):
        out = out + jnp.where(head_of_lane == h, o[h * t:(h + 1) * t, :], 0.0)
    o_ref[...] = out.astype(BF16)


def _attn_decode_call(q, kn, vn, cache_k, cache_v, rvec, *, t):
    nb, w = cache_k.shape[0], cache_k.shape[1]
    assert w == BAND_PAST
    kpad = -(-(w + t) // LANES_V7X) * LANES_V7X
    assert kpad + t <= REL_RING
    new_spec = pl.BlockSpec((t, D_MODEL), lambda i: (i, 0))
    cache_spec = pl.BlockSpec((1, w, D_MODEL), lambda i: (i, 0, 0))
    return pl.pallas_call(
        _attn_decode_kernel,
        grid=(nb,),
        in_specs=[new_spec, new_spec, new_spec, cache_spec, cache_spec, _const_spec(rvec.shape)],
        out_specs=new_spec,
        out_shape=jax.ShapeDtypeStruct((nb * t, D_MODEL), BF16),
        scratch_shapes=[pltpu.VMEM((N_HEADS * t, kpad), F32)],
        compiler_params=_params(),
        name="attn_decode",
    )(q, kn, vn, cache_k, cache_v, rvec)


def _rel_vector(table):
    m = jnp.arange(REL_RING)
    d = jnp.where(m < K_BLOCK, m, m - REL_RING)
    idx = jnp.clip(BAND_PAST - d, -REL_MAX, REL_MAX) + REL_MAX
    return table[idx].T.astype(F32)


def _trunk(x, conv_prev, caches, wts, *, tm, n_sub, proj_tm, keep):
    s = x.shape[0]
    ln = lambda l: (wts["ln_mix_g"][l][None], wts["ln_mix_b"][l][None],
                    wts["ln_ffn_g"][l][None], wts["ln_ffn_b"][l][None])
    stream_len = 0 if conv_prev is None else s // conv_prev.shape[1]
    pairs = caches is None
    ztails = []
    for l in range(N_A):
        mix = (wts["w_in_a"][l], wts["conv_w_a"][l], wts["w_out_a"][l])
        if stream_len:
            p2 = jnp.pad(conv_prev[l], ((0, 0), (0, stream_len - 2), (0, 0))).reshape(s, D_MODEL)
            p1 = jnp.pad(conv_prev[l][:, 1:], ((0, 0), (0, stream_len - 1), (0, 0))).reshape(s, D_MODEL)
            mix = (p1, p2) + mix
        x, zt = _layer_call(x, mix, ln(l), wts["w_up"][l], wts["w_down"][l],
                            conv=True, stream_len=stream_len, pairs=False, tm=tm, n_sub=n_sub)
        ztails.append(zt)

    q_scale = HEAD_DIM ** -0.5 * LOG2E
    q, k, kf, v, vf = _proj_call(x, (wts["w_q_b"][0], wts["w_k"], wts["w_v"]),
                                 (q_scale, 1.0, 1.0), (False, True, True),
                                 tm=proj_tm, keep=keep, pairs=pairs)
    for j in range(DEPTH - N_A):
        l = N_A + j
        if j > 0:
            (q,) = _proj_call(x, (wts["w_q_b"][j],), (q_scale,), (False,),
                              tm=proj_tm, keep=keep, pairs=pairs)
        rvec = _rel_vector(wts["rel_bias_b"][j])
        if caches is None:
            a = _attn_prompt_call(q, k, v, rvec)
        else:
            a = _attn_decode_call(q, k, v, caches[0], caches[1], rvec, t=s // caches[0].shape[0])
        x, = _layer_call(x, (a, wts["w_o_b"][j]), ln(l), wts["w_up"][l], wts["w_down"][l],
                         conv=False, stream_len=0, pairs=pairs, tm=tm, n_sub=n_sub)
    return x, ztails, kf, vf


def kernel(x_prompt, x_sample, cache_conv, cache_k, cache_v, ln_mix_g, ln_mix_b, ln_ffn_g, ln_ffn_b,
           w_up, w_down, w_in_a, conv_w_a, w_out_a, w_k, w_v, w_q_b, w_o_b, rel_bias_b):
    assert x_prompt.shape[0] == 1
    wts = dict(
        ln_mix_g=ln_mix_g, ln_mix_b=ln_mix_b, ln_ffn_g=ln_ffn_g, ln_ffn_b=ln_ffn_b,
        w_up=w_up.astype(BF16), w_down=w_down.astype(BF16), w_in_a=w_in_a.astype(BF16),
        conv_w_a=conv_w_a, w_out_a=w_out_a.astype(BF16), w_k=w_k.astype(BF16), w_v=w_v.astype(BF16),
        w_q_b=w_q_b.astype(BF16), w_o_b=w_o_b.astype(BF16), rel_bias_b=rel_bias_b)

    seq = x_prompt.shape[1]
    keep = min(BAND_PAST, seq)
    yp, ztp, kfp, vfp = _trunk(x_prompt[0], None, None, wts, tm=512, n_sub=2, proj_tm=1024, keep=keep)
    conv_prompt = jnp.stack([zt[SUBLANES_V7X - 2:] for zt in ztp])[:, None]
    k_prompt = kfp.reshape(1, keep, N_HEADS, HEAD_DIM)
    v_prompt = vfp.reshape(1, keep, N_HEADS, HEAD_DIM)

    nb, t = x_sample.shape[0], x_sample.shape[1]
    w = cache_k.shape[1]
    ns = nb * t
    ys, zts, kfs, vfs = _trunk(
        x_sample.reshape(ns, D_MODEL), cache_conv,
        (cache_k.reshape(nb, w, D_MODEL), cache_v.reshape(nb, w, D_MODEL)),
        wts, tm=ns, n_sub=1, proj_tm=ns, keep=ns)
    conv_sample = jnp.stack([zt.reshape(nb, t, D_MODEL)[:, t - 2:] for zt in zts])
    k_sample = kfs.reshape(nb, t, N_HEADS, HEAD_DIM)
    v_sample = vfs.reshape(nb, t, N_HEADS, HEAD_DIM)
    return (yp[None], ys.reshape(nb, t, D_MODEL), conv_prompt, k_prompt, v_prompt,
            conv_sample, k_sample, v_sample)
```

```python
import dataclasses
import functools
import math

import jax
import jax.numpy as jnp
from jax import lax
from jax.experimental import pallas as pl
from jax.experimental.pallas import tpu as pltpu

D_MODEL = 1024
N_HEADS = 16
HEAD_DIM = 64
DEPTH = 4
N_A = 2
CHUNK = 64
N_PREV_CHUNKS = 8
BAND_PAST = N_PREV_CHUNKS * CHUNK
REL_MAX = 128
ALPHA = (2.0 * DEPTH) ** 0.25
LN_EPS = 1e-5
LOG2E = math.log2(math.e)

LANES_V7X = 128
SUBLANES_V7X = 8
VMEM_BYTES_V7X = 64 * 1024 * 1024
VMEM_LIMIT_BYTES = VMEM_BYTES_V7X - 8 * 1024 * 1024

PAIR_W = LANES_V7X
N_PAIRS = D_MODEL // PAIR_W
Q_BLOCK = 4 * CHUNK
K_BLOCK = BAND_PAST + Q_BLOCK
ATT_STEP = 2 * Q_BLOCK
REL_RING = 1024
MASK_VALUE = -1e30

BF16 = jnp.bfloat16
F32 = jnp.float32


def _const_spec(shape):
    zeros = (0,) * len(shape)
    return pl.BlockSpec(shape, lambda i: zeros, pipeline_mode=pl.Buffered(1))


@dataclasses.dataclass(frozen=True)
class _Weight:
    array: jax.Array
    layer: int | None = None

    @property
    def shape(self):
        return self.array.shape if self.layer is None else self.array.shape[1:]

    def spec(self):
        if self.layer is None:
            return _const_spec(self.array.shape)
        index = (self.layer,) + (0,) * len(self.shape)
        return pl.BlockSpec((None,) + self.shape, lambda i: index, pipeline_mode=pl.Buffered(1))


def _params():
    return pltpu.CompilerParams(dimension_semantics=("arbitrary",), vmem_limit_bytes=VMEM_LIMIT_BYTES)


def _dot(a, b):
    return jnp.dot(a, b, preferred_element_type=F32)


def _dot_nt(a, b):
    return lax.dot_general(a, b, (((1,), (1,)), ((), ())), preferred_element_type=F32)


def _layer_norm(x, g, b):
    mu = jnp.mean(x, axis=-1, keepdims=True)
    xc = x - mu
    var = jnp.mean(xc * xc, axis=-1, keepdims=True)
    return xc * lax.rsqrt(var + LN_EPS) * g + b


def _store_pairs(ref, val):
    for p in range(N_PAIRS):
        ref[p] = val[:, p * PAIR_W:(p + 1) * PAIR_W].astype(BF16)


@dataclasses.dataclass(frozen=True)
class _LayerCfg:
    conv: bool
    stream_len: int
    pairs: bool
    tm: int
    n_sub: int

    @property
    def streams(self):
        return self.stream_len > 0


def _layer_kernel(*refs, cfg):
    it = iter(refs)
    x_ref = next(it)
    if cfg.conv:
        if cfg.streams:
            p1_ref, p2_ref = next(it), next(it)
        w_in_ref, cw_ref, w_mix_ref = next(it), next(it), next(it)
    else:
        a_ref, w_mix_ref = next(it), next(it)
    g1_ref, b1_ref, g2_ref, b2_ref = next(it), next(it), next(it), next(it)
    w_up_ref, w_down_ref = next(it), next(it)
    y_ref = next(it)
    if cfg.conv:
        ztail_ref = next(it)
        if not cfg.streams:
            carry_ref = next(it)

    if cfg.conv and not cfg.streams:
        @pl.when(pl.program_id(0) == 0)
        def _():
            carry_ref[...] = jnp.zeros_like(carry_ref)
        prev = carry_ref[...]

    sm = cfg.tm // cfg.n_sub
    subs = range(cfg.n_sub)
    rows = [slice(sub * sm, (sub + 1) * sm) for sub in subs]
    x = [x_ref[r, :] for r in rows]
    if cfg.conv:
        bch = [_dot(x[i].astype(BF16), w_in_ref[...]) for i in subs]
        row = lax.broadcasted_iota(jnp.int32, (sm, D_MODEL), 0)
        cw = cw_ref[...]
        mix_in = []
        for i in subs:
            b_gate = bch[i][:, :D_MODEL]
            z = bch[i][:, D_MODEL:2 * D_MODEL] * bch[i][:, 2 * D_MODEL:]
            r1 = pltpu.roll(z, 1, 0)
            r2 = pltpu.roll(z, 2, 0)
            if cfg.streams:
                rm = row & (cfg.stream_len - 1)
                zm1 = jnp.where(rm == 0, p1_ref[rows[i], :], r1)
                zm2 = jnp.where(rm < 2, p2_ref[rows[i], :], r2)
                ztail_ref[rows[i], :] = z
            else:
                pm1 = prev[SUBLANES_V7X - 1:SUBLANES_V7X, :]
                pm2 = prev[SUBLANES_V7X - 2:SUBLANES_V7X - 1, :]
                zm1 = jnp.where(row == 0, pm1, r1)
                zm2 = jnp.where(row == 0, pm2, jnp.where(row == 1, pm1, r2))
                prev = z[sm - SUBLANES_V7X:, :]
            yc = cw[0:1, :] * zm2 + cw[1:2, :] * zm1 + cw[2:3, :] * z
            mix_in.append((b_gate * yc).astype(BF16))
        if not cfg.streams:
            carry_ref[...] = prev
            ztail_ref[...] = prev
    elif cfg.pairs:
        mix_in = [jnp.concatenate([a_ref[p, r, :] for p in range(N_PAIRS)], axis=-1) for r in rows]
    else:
        mix_in = [a_ref[r, :] for r in rows]
    h = [_dot(mix_in[i], w_mix_ref[...]) for i in subs]
    x1 = [_layer_norm(ALPHA * x[i] + h[i], g1_ref[...], b1_ref[...]) for i in subs]
    u = [_dot(x1[i].astype(BF16), w_up_ref[...]) for i in subs]
    u = [jnp.maximum(u[i], 0.0) for i in subs]
    m = [_dot((u[i] * u[i]).astype(BF16), w_down_ref[...]) for i in subs]
    for i in subs:
        y_ref[rows[i], :] = _layer_norm(ALPHA * x1[i] + m[i], g2_ref[...], b2_ref[...])


def _layer_call(x, mix_args, ln, w_up, w_down, *, conv, stream_len, pairs, tm, n_sub):
    s = x.shape[0]
    assert s % tm == 0 and tm % n_sub == 0
    assert stream_len & (stream_len - 1) == 0 and (stream_len == 0 or (tm // n_sub) % stream_len == 0)
    n = s // tm
    cfg = _LayerCfg(conv=conv, stream_len=stream_len, pairs=pairs, tm=tm, n_sub=n_sub)
    streams = cfg.streams
    row_spec = pl.BlockSpec((tm, D_MODEL), lambda i: (i, 0))
    pair_spec = pl.BlockSpec((N_PAIRS, tm, PAIR_W), lambda i: (0, i, 0))
    in_specs = [row_spec]
    args = [x]
    if conv:
        if streams:
            p1, p2, w_in, cw, w_out = mix_args
            in_specs += [row_spec, row_spec]
            args += [p1, p2]
        else:
            w_in, cw, w_out = mix_args
        in_specs += [w_in.spec(), cw.spec(), w_out.spec()]
        args += [w_in.array, cw.array, w_out.array]
    else:
        a, w_o = mix_args
        in_specs += [pair_spec if pairs else row_spec, w_o.spec()]
        args += [a, w_o.array]
    in_specs += [_const_spec((1, D_MODEL))] * 4 + [w_up.spec(), w_down.spec()]
    args += list(ln) + [w_up.array, w_down.array]

    out_shape = [jax.ShapeDtypeStruct((s, D_MODEL), F32)]
    out_specs = [row_spec]
    scratch = []
    if conv:
        if streams:
            out_shape.append(jax.ShapeDtypeStruct((s, D_MODEL), F32))
            out_specs.append(row_spec)
        else:
            out_shape.append(jax.ShapeDtypeStruct((SUBLANES_V7X, D_MODEL), F32))
            out_specs.append(pl.BlockSpec((SUBLANES_V7X, D_MODEL), lambda i: (0, 0)))
            scratch.append(pltpu.VMEM((SUBLANES_V7X, D_MODEL), F32))
    return pl.pallas_call(
        functools.partial(_layer_kernel, cfg=cfg),
        grid=(n,),
        in_specs=in_specs,
        out_specs=out_specs,
        out_shape=out_shape,
        scratch_shapes=scratch,
        compiler_params=_params(),
        name="layer_conv" if conv else "layer_attn",
    )(*args)


def _proj_kernel(*refs, scales, tails, tail_rows, pairs):
    n = len(scales)
    x_ref = refs[0]
    w_refs = refs[1:1 + n]
    out_refs = list(refs[1 + n:])
    xb = x_ref[...].astype(BF16)
    tm = xb.shape[0]
    for j in range(n):
        pj = _dot(xb, w_refs[j][...])
        if scales[j] != 1.0:
            pj = pj * scales[j]
        if pairs:
            _store_pairs(out_refs.pop(0), pj)
        else:
            out_refs.pop(0)[...] = pj.astype(BF16)
        if tails[j]:
            out_refs.pop(0)[...] = pj[tm - tail_rows:, :]


def _proj_call(x, ws, scales, tails, *, tm, keep, pairs):
    s = x.shape[0]
    assert s % tm == 0
    n = s // tm
    tail_rows = min(tm, keep)
    assert keep % tail_rows == 0
    n_tail_blocks = keep // tail_rows
    row_spec = pl.BlockSpec((tm, D_MODEL), lambda i: (i, 0))
    pair_spec = pl.BlockSpec((N_PAIRS, tm, PAIR_W), lambda i: (0, i, 0))
    tail_spec = pl.BlockSpec((tail_rows, D_MODEL),
                             lambda i: (jnp.maximum(i - (n - n_tail_blocks), 0), 0))
    out_shape, out_specs = [], []
    for t in tails:
        if pairs:
            out_shape.append(jax.ShapeDtypeStruct((N_PAIRS, s, PAIR_W), BF16))
            out_specs.append(pair_spec)
        else:
            out_shape.append(jax.ShapeDtypeStruct((s, D_MODEL), BF16))
            out_specs.append(row_spec)
        if t:
            out_shape.append(jax.ShapeDtypeStruct((keep, D_MODEL), F32))
            out_specs.append(tail_spec)
    return pl.pallas_call(
        functools.partial(_proj_kernel, scales=tuple(scales), tails=tuple(tails),
                          tail_rows=tail_rows, pairs=pairs),
        grid=(n,),
        in_specs=[row_spec] + [w.spec() for w in ws],
        out_specs=out_specs,
        out_shape=out_shape,
        compiler_params=_params(),
        name="proj",
    )(x, *[w.array for w in ws])


def _rel_bias_rows(rvec_ref, h, rows, valid):
    row = jnp.broadcast_to(rvec_ref[h:h + 1, :], (rows, REL_RING))
    t = pltpu.roll(row, 0, 1, stride=1, stride_axis=0)
    return jnp.where(valid, t * LOG2E, MASK_VALUE)


def _softmax_pv(s, v_parts):
    m = jnp.max(s, axis=-1, keepdims=True)
    e = jnp.exp2(s - m)
    l = jnp.sum(e, axis=-1, keepdims=True)
    eb = e.astype(BF16)
    o = None
    off = 0
    for v in v_parts:
        part = _dot(eb[:, off:off + v.shape[0]], v)
        o = part if o is None else o + part
        off += v.shape[0]
    return o / l


def _attn_prompt_kernel(q_ref, kp_ref, kc_ref, vp_ref, vc_ref, rvec_ref, o_ref, bias_ref):
    step = pl.program_id(0)
    low = lax.broadcasted_iota(jnp.int32, (Q_BLOCK, PAIR_W), 1) < HEAD_DIM

    def build_bias(first_key):
        qi = lax.broadcasted_iota(jnp.int32, (Q_BLOCK, REL_RING), 0)
        kj = lax.broadcasted_iota(jnp.int32, (Q_BLOCK, REL_RING), 1)
        start = (qi >> 6) << 6
        valid = (kj >= start) & (kj < start + (BAND_PAST + CHUNK)) & (kj >= first_key)
        for h in range(N_HEADS):
            bias_ref[h * Q_BLOCK:(h + 1) * Q_BLOCK, :] = (
                _rel_bias_rows(rvec_ref, h, Q_BLOCK, valid)[:, :K_BLOCK])

    for blk in range(ATT_STEP // Q_BLOCK):
        past_lo = blk * Q_BLOCK
        cur_hi = (blk + 1) * Q_BLOCK
        rebuild = (step == 0) if blk > 0 else (step <= 1)
        pl.when(rebuild)(functools.partial(
            build_bias, jnp.where(step == 0, BAND_PAST - past_lo, 0)))

        q_rows = slice(blk * Q_BLOCK, (blk + 1) * Q_BLOCK)

        def scores(p):
            qb = q_ref[p, q_rows, :]
            qs = jnp.concatenate([jnp.where(low, qb, jnp.zeros_like(qb)),
                                  jnp.where(low, jnp.zeros_like(qb), qb)], axis=0)
            s = jnp.concatenate([_dot_nt(qs, kp_ref[p, past_lo:, :]),
                                 _dot_nt(qs, kc_ref[p, :cur_hi, :])], axis=-1)
            return s + bias_ref[2 * p * Q_BLOCK:2 * (p + 1) * Q_BLOCK, :]

        def finish(p, s):
            o = _softmax_pv(s, (vp_ref[p, past_lo:, :], vc_ref[p, :cur_hi, :]))
            o_ref[p, q_rows, :] = jnp.where(low, o[:Q_BLOCK], o[Q_BLOCK:]).astype(BF16)

        s_next = scores(0)
        for p in range(N_PAIRS):
            s_cur = s_next
            if p + 1 < N_PAIRS:
                s_next = scores(p + 1)
            finish(p, s_cur)


def _attn_prompt_call(q, k, v, rvec):
    s = q.shape[1]
    assert s % ATT_STEP == 0 and ATT_STEP >= BAND_PAST
    n = s // ATT_STEP
    cur = pl.BlockSpec((N_PAIRS, ATT_STEP, PAIR_W), lambda i: (0, i, 0))
    prev = pl.BlockSpec((N_PAIRS, ATT_STEP, PAIR_W), lambda i: (0, jnp.maximum(i - 1, 0), 0))
    return pl.pallas_call(
        _attn_prompt_kernel,
        grid=(n,),
        in_specs=[cur, prev, cur, prev, cur, _const_spec(rvec.shape)],
        out_specs=cur,
        out_shape=jax.ShapeDtypeStruct((N_PAIRS, s, PAIR_W), BF16),
        scratch_shapes=[pltpu.VMEM((N_HEADS * Q_BLOCK, K_BLOCK), F32)],
        compiler_params=_params(),
        name="attn_prompt",
    )(q, k, k, v, v, rvec)


def _attn_decode_kernel(q_ref, kn_ref, vn_ref, kc_ref, vc_ref, rvec_ref, o_ref, bias_ref):
    t = q_ref.shape[0]
    w = kc_ref.shape[1]
    kpad = bias_ref.shape[1]

    @pl.when(pl.program_id(0) == 0)
    def _():
        kj = lax.broadcasted_iota(jnp.int32, (t, REL_RING), 1)
        valid = kj < w + t
        for h in range(N_HEADS):
            bias_ref[h * t:(h + 1) * t, :] = _rel_bias_rows(rvec_ref, h, t, valid)[:, :kpad]

    q = q_ref[...]
    head_of_lane = lax.broadcasted_iota(jnp.int32, (t, D_MODEL), 1) >> 6
    qs = jnp.concatenate(
        [jnp.where(head_of_lane == h, q, jnp.zeros_like(q)) for h in range(N_HEADS)], axis=0)
    zeros = jnp.zeros((kpad - w - t, D_MODEL), BF16)
    kk = jnp.concatenate([kc_ref[0], kn_ref[...], zeros], axis=0)
    vv = jnp.concatenate([vc_ref[0], vn_ref[...], zeros], axis=0)
    o = _softmax_pv(_dot_nt(qs, kk) + bias_ref[...], (vv,))
    out = jnp.zeros((t, D_MODEL), F32)
    for h in range(N_HEADS):
        out = out + jnp.where(head_of_lane == h, o[h * t:(h + 1) * t, :], 0.0)
    o_ref[...] = out.astype(BF16)


def _attn_decode_call(q, kn, vn, cache_k, cache_v, rvec, *, t):
    nb, w = cache_k.shape[0], cache_k.shape[1]
    assert w == BAND_PAST
    kpad = -(-(w + t) // LANES_V7X) * LANES_V7X
    assert kpad + t <= REL_RING
    new_spec = pl.BlockSpec((t, D_MODEL), lambda i: (i, 0))
    cache_spec = pl.BlockSpec((1, w, D_MODEL), lambda i: (i, 0, 0))
    return pl.pallas_call(
        _attn_decode_kernel,
        grid=(nb,),
        in_specs=[new_spec, new_spec, new_spec, cache_spec, cache_spec, _const_spec(rvec.shape)],
        out_specs=new_spec,
        out_shape=jax.ShapeDtypeStruct((nb * t, D_MODEL), BF16),
        scratch_shapes=[pltpu.VMEM((N_HEADS * t, kpad), F32)],
        compiler_params=_params(),
        name="attn_decode",
    )(q, kn, vn, cache_k, cache_v, rvec)


def _rel_vector(table):
    m = jnp.arange(REL_RING)
    d = jnp.where(m < K_BLOCK, m, m - REL_RING)
    idx = jnp.clip(BAND_PAST - d, -REL_MAX, REL_MAX) + REL_MAX
    return table[idx].T.astype(F32)


def _trunk(x, conv_prev, caches, wts, *, tm, n_sub, proj_tm, keep):
    s = x.shape[0]
    ln = lambda l: (wts["ln_mix_g"][l][None], wts["ln_mix_b"][l][None],
                    wts["ln_ffn_g"][l][None], wts["ln_ffn_b"][l][None])
    stream_len = 0 if conv_prev is None else s // conv_prev.shape[1]
    pairs = caches is None
    ztails = []
    mlp = lambda l: (_Weight(wts["w_up"], l), _Weight(wts["w_down"], l))
    for l in range(N_A):
        mix = (_Weight(wts["w_in_a"], l), _Weight(wts["conv_w_a"], l), _Weight(wts["w_out_a"], l))
        if stream_len:
            p2 = jnp.pad(conv_prev[l], ((0, 0), (0, stream_len - 2), (0, 0))).reshape(s, D_MODEL)
            p1 = jnp.pad(conv_prev[l][:, 1:], ((0, 0), (0, stream_len - 1), (0, 0))).reshape(s, D_MODEL)
            mix = (p1, p2) + mix
        x, zt = _layer_call(x, mix, ln(l), *mlp(l),
                            conv=True, stream_len=stream_len, pairs=False, tm=tm, n_sub=n_sub)
        ztails.append(zt)

    q_scale = HEAD_DIM ** -0.5 * LOG2E
    q, k, kf, v, vf = _proj_call(
        x, (_Weight(wts["w_q_b"], 0), _Weight(wts["w_k"]), _Weight(wts["w_v"])),
        (q_scale, 1.0, 1.0), (False, True, True), tm=proj_tm, keep=keep, pairs=pairs)
    for j in range(DEPTH - N_A):
        l = N_A + j
        if j > 0:
            (q,) = _proj_call(x, (_Weight(wts["w_q_b"], j),), (q_scale,), (False,),
                              tm=proj_tm, keep=keep, pairs=pairs)
        rvec = _rel_vector(wts["rel_bias_b"][j])
        if caches is None:
            a = _attn_prompt_call(q, k, v, rvec)
        else:
            a = _attn_decode_call(q, k, v, caches[0], caches[1], rvec, t=s // caches[0].shape[0])
        x, = _layer_call(x, (a, _Weight(wts["w_o_b"], j)), ln(l), *mlp(l),
                         conv=False, stream_len=0, pairs=pairs, tm=tm, n_sub=n_sub)
    return x, ztails, kf, vf


def kernel(x_prompt, x_sample, cache_conv, cache_k, cache_v, ln_mix_g, ln_mix_b, ln_ffn_g, ln_ffn_b,
           w_up, w_down, w_in_a, conv_w_a, w_out_a, w_k, w_v, w_q_b, w_o_b, rel_bias_b):
    assert x_prompt.shape[0] == 1
    wts = dict(
        ln_mix_g=ln_mix_g, ln_mix_b=ln_mix_b, ln_ffn_g=ln_ffn_g, ln_ffn_b=ln_ffn_b,
        w_up=w_up.astype(BF16), w_down=w_down.astype(BF16), w_in_a=w_in_a.astype(BF16),
        conv_w_a=conv_w_a, w_out_a=w_out_a.astype(BF16), w_k=w_k.astype(BF16), w_v=w_v.astype(BF16),
        w_q_b=w_q_b.astype(BF16), w_o_b=w_o_b.astype(BF16), rel_bias_b=rel_bias_b)

    seq = x_prompt.shape[1]
    keep = min(BAND_PAST, seq)
    yp, ztp, kfp, vfp = _trunk(x_prompt[0], None, None, wts, tm=512, n_sub=2, proj_tm=1024, keep=keep)
    conv_prompt = jnp.stack([zt[SUBLANES_V7X - 2:] for zt in ztp])[:, None]
    k_prompt = kfp.reshape(1, keep, N_HEADS, HEAD_DIM)
    v_prompt = vfp.reshape(1, keep, N_HEADS, HEAD_DIM)

    nb, t = x_sample.shape[0], x_sample.shape[1]
    w = cache_k.shape[1]
    ns = nb * t
    ys, zts, kfs, vfs = _trunk(
        x_sample.reshape(ns, D_MODEL), cache_conv,
        (cache_k.reshape(nb, w, D_MODEL).astype(BF16), cache_v.reshape(nb, w, D_MODEL).astype(BF16)),
        wts, tm=ns, n_sub=1, proj_tm=ns, keep=ns)
    conv_sample = jnp.stack([zt.reshape(nb, t, D_MODEL)[:, t - 2:] for zt in zts])
    k_sample = kfs.reshape(nb, t, N_HEADS, HEAD_DIM)
    v_sample = vfs.reshape(nb, t, N_HEADS, HEAD_DIM)
    return (yp[None], ys.reshape(nb, t, D_MODEL), conv_prompt, k_prompt, v_prompt,
            conv_sample, k_sample, v_sample)
```

```python
import dataclasses
import functools
import math

import jax
import jax.numpy as jnp
from jax import lax
from jax.experimental import pallas as pl
from jax.experimental.pallas import tpu as pltpu

D_MODEL = 1024
N_HEADS = 16
HEAD_DIM = 64
DEPTH = 4
N_A = 2
CHUNK = 64
N_PREV_CHUNKS = 8
BAND_PAST = N_PREV_CHUNKS * CHUNK
REL_MAX = 128
ALPHA = (2.0 * DEPTH) ** 0.25
LN_EPS = 1e-5
LOG2E = math.log2(math.e)

LANES_V7X = 128
SUBLANES_V7X = 8
VMEM_BYTES_V7X = 64 * 1024 * 1024
VMEM_LIMIT_BYTES = VMEM_BYTES_V7X - 8 * 1024 * 1024

PAIR_W = LANES_V7X
N_PAIRS = D_MODEL // PAIR_W
Q_BLOCK = 4 * CHUNK
K_BLOCK = BAND_PAST + Q_BLOCK
ATT_STEP = 2 * Q_BLOCK
SUBTILE_GROUP = 2
REL_RING = 1024
MASK_VALUE = -1e30

BF16 = jnp.bfloat16
F32 = jnp.float32


def _const_spec(shape):
    zeros = (0,) * len(shape)
    return pl.BlockSpec(shape, lambda i: zeros, pipeline_mode=pl.Buffered(1))


@dataclasses.dataclass(frozen=True)
class _Weight:
    array: jax.Array
    layer: int | None = None

    @property
    def shape(self):
        return self.array.shape if self.layer is None else self.array.shape[1:]

    def spec(self):
        if self.layer is None:
            return _const_spec(self.array.shape)
        index = (self.layer,) + (0,) * len(self.shape)
        return pl.BlockSpec((None,) + self.shape, lambda i: index, pipeline_mode=pl.Buffered(1))


def _params():
    return pltpu.CompilerParams(dimension_semantics=("arbitrary",), vmem_limit_bytes=VMEM_LIMIT_BYTES)


def _dot(a, b):
    return jnp.dot(a, b, preferred_element_type=F32)


def _dot_nt(a, b):
    return lax.dot_general(a, b, (((1,), (1,)), ((), ())), preferred_element_type=F32)


def _layer_norm(x, g, b):
    mu = jnp.mean(x, axis=-1, keepdims=True)
    xc = x - mu
    var = jnp.mean(xc * xc, axis=-1, keepdims=True)
    return xc * lax.rsqrt(var + LN_EPS) * g + b


def _store_pairs(ref, val):
    for p in range(N_PAIRS):
        ref[p] = val[:, p * PAIR_W:(p + 1) * PAIR_W].astype(BF16)


@dataclasses.dataclass(frozen=True)
class _LayerCfg:
    conv: bool
    stream_len: int
    pairs: bool
    tm: int
    n_sub: int

    @property
    def streams(self):
        return self.stream_len > 0


def _layer_kernel(*refs, cfg):
    it = iter(refs)
    x_ref = next(it)
    if cfg.conv:
        if cfg.streams:
            p1_ref, p2_ref = next(it), next(it)
        w_in_ref, cw_ref, w_mix_ref = next(it), next(it), next(it)
    else:
        a_ref, w_mix_ref = next(it), next(it)
    g1_ref, b1_ref, g2_ref, b2_ref = next(it), next(it), next(it), next(it)
    w_up_ref, w_down_ref = next(it), next(it)
    y_ref = next(it)
    if cfg.conv:
        ztail_ref = next(it)
        if not cfg.streams:
            carry_ref = next(it)

    if cfg.conv and not cfg.streams:
        @pl.when(pl.program_id(0) == 0)
        def _():
            carry_ref[...] = jnp.zeros_like(carry_ref)
        prev = carry_ref[...]

    sm = cfg.tm // cfg.n_sub
    group = min(cfg.n_sub, SUBTILE_GROUP)
    for g0 in range(0, cfg.n_sub, group):
        subs = range(group)
        rows = [slice((g0 + i) * sm, (g0 + i + 1) * sm) for i in subs]
        x = [x_ref[r, :] for r in rows]
        if cfg.conv:
            bch = [_dot(x[i].astype(BF16), w_in_ref[...]) for i in subs]
            row = lax.broadcasted_iota(jnp.int32, (sm, D_MODEL), 0)
            cw = cw_ref[...]
            mix_in = []
            for i in subs:
                b_gate = bch[i][:, :D_MODEL]
                z = bch[i][:, D_MODEL:2 * D_MODEL] * bch[i][:, 2 * D_MODEL:]
                r1 = pltpu.roll(z, 1, 0)
                r2 = pltpu.roll(z, 2, 0)
                if cfg.streams:
                    rm = row & (cfg.stream_len - 1)
                    zm1 = jnp.where(rm == 0, p1_ref[rows[i], :], r1)
                    zm2 = jnp.where(rm < 2, p2_ref[rows[i], :], r2)
                    ztail_ref[rows[i], :] = z
                else:
                    pm1 = prev[SUBLANES_V7X - 1:SUBLANES_V7X, :]
                    pm2 = prev[SUBLANES_V7X - 2:SUBLANES_V7X - 1, :]
                    zm1 = jnp.where(row == 0, pm1, r1)
                    zm2 = jnp.where(row == 0, pm2, jnp.where(row == 1, pm1, r2))
                    prev = z[sm - SUBLANES_V7X:, :]
                yc = cw[0:1, :] * zm2 + cw[1:2, :] * zm1 + cw[2:3, :] * z
                mix_in.append((b_gate * yc).astype(BF16))
        elif cfg.pairs:
            mix_in = [jnp.concatenate([a_ref[p, r, :] for p in range(N_PAIRS)], axis=-1) for r in rows]
        else:
            mix_in = [a_ref[r, :] for r in rows]
        h = [_dot(mix_in[i], w_mix_ref[...]) for i in subs]
        x1 = [_layer_norm(ALPHA * x[i] + h[i], g1_ref[...], b1_ref[...]) for i in subs]
        u = [_dot(x1[i].astype(BF16), w_up_ref[...]) for i in subs]
        u = [jnp.maximum(u[i], 0.0) for i in subs]
        m = [_dot((u[i] * u[i]).astype(BF16), w_down_ref[...]) for i in subs]
        for i in subs:
            y_ref[rows[i], :] = _layer_norm(ALPHA * x1[i] + m[i], g2_ref[...], b2_ref[...])
    if cfg.conv and not cfg.streams:
        carry_ref[...] = prev
        ztail_ref[...] = prev


def _layer_call(x, mix_args, ln, w_up, w_down, *, conv, stream_len, pairs, tm, n_sub):
    s = x.shape[0]
    assert s % tm == 0 and tm % n_sub == 0
    assert stream_len & (stream_len - 1) == 0 and (stream_len == 0 or (tm // n_sub) % stream_len == 0)
    n = s // tm
    cfg = _LayerCfg(conv=conv, stream_len=stream_len, pairs=pairs, tm=tm, n_sub=n_sub)
    streams = cfg.streams
    row_spec = pl.BlockSpec((tm, D_MODEL), lambda i: (i, 0))
    pair_spec = pl.BlockSpec((N_PAIRS, tm, PAIR_W), lambda i: (0, i, 0))
    in_specs = [row_spec]
    args = [x]
    if conv:
        if streams:
            p1, p2, w_in, cw, w_out = mix_args
            in_specs += [row_spec, row_spec]
            args += [p1, p2]
        else:
            w_in, cw, w_out = mix_args
        in_specs += [w_in.spec(), cw.spec(), w_out.spec()]
        args += [w_in.array, cw.array, w_out.array]
    else:
        a, w_o = mix_args
        in_specs += [pair_spec if pairs else row_spec, w_o.spec()]
        args += [a, w_o.array]
    in_specs += [_const_spec((1, D_MODEL))] * 4 + [w_up.spec(), w_down.spec()]
    args += list(ln) + [w_up.array, w_down.array]

    out_shape = [jax.ShapeDtypeStruct((s, D_MODEL), F32)]
    out_specs = [row_spec]
    scratch = []
    if conv:
        if streams:
            out_shape.append(jax.ShapeDtypeStruct((s, D_MODEL), F32))
            out_specs.append(row_spec)
        else:
            out_shape.append(jax.ShapeDtypeStruct((SUBLANES_V7X, D_MODEL), F32))
            out_specs.append(pl.BlockSpec((SUBLANES_V7X, D_MODEL), lambda i: (0, 0)))
            scratch.append(pltpu.VMEM((SUBLANES_V7X, D_MODEL), F32))
    return pl.pallas_call(
        functools.partial(_layer_kernel, cfg=cfg),
        grid=(n,),
        in_specs=in_specs,
        out_specs=out_specs,
        out_shape=out_shape,
        scratch_shapes=scratch,
        compiler_params=_params(),
        name="layer_conv" if conv else "layer_attn",
    )(*args)


def _proj_kernel(*refs, scales, tails, layouts, tail_rows, first_tail_step):
    n = len(scales)
    x_ref = refs[0]
    w_refs = refs[1:1 + n]
    out_refs = list(refs[1 + n:])
    xb = x_ref[...].astype(BF16)
    tm = xb.shape[0]
    for j in range(n):
        if layouts[j] == "cols":
            pt = _dot_nt(w_refs[j][...], xb)
            out_refs.pop(0)[...] = pt.astype(BF16)
            if tails[j]:
                tail_ref = out_refs.pop(0)

                @pl.when(pl.program_id(0) >= first_tail_step)
                def _(tail_ref=tail_ref, pt=pt):
                    tail_ref[...] = pt[:, tm - tail_rows:].T
            continue
        pj = _dot(xb, w_refs[j][...])
        if scales[j] != 1.0:
            pj = pj * scales[j]
        if layouts[j] == "pairs":
            _store_pairs(out_refs.pop(0), pj)
        else:
            out_refs.pop(0)[...] = pj.astype(BF16)
        if tails[j]:
            out_refs.pop(0)[...] = pj[tm - tail_rows:, :]


def _proj_call(x, ws, scales, tails, layouts, *, tm, keep):
    s = x.shape[0]
    assert s % tm == 0
    n = s // tm
    tail_rows = min(tm, keep)
    assert keep % tail_rows == 0
    n_tail_blocks = keep // tail_rows
    first_tail_step = n - n_tail_blocks
    row_spec = pl.BlockSpec((tm, D_MODEL), lambda i: (i, 0))
    specs = {
        "rows": ((s, D_MODEL), row_spec),
        "pairs": ((N_PAIRS, s, PAIR_W), pl.BlockSpec((N_PAIRS, tm, PAIR_W), lambda i: (0, i, 0))),
        "cols": ((D_MODEL, s), pl.BlockSpec((D_MODEL, tm), lambda i: (0, i))),
    }
    tail_spec = pl.BlockSpec((tail_rows, D_MODEL),
                             lambda i: (jnp.maximum(i - first_tail_step, 0), 0))
    out_shape, out_specs = [], []
    for t, layout in zip(tails, layouts):
        shape, spec = specs[layout]
        out_shape.append(jax.ShapeDtypeStruct(shape, BF16))
        out_specs.append(spec)
        if t:
            out_shape.append(jax.ShapeDtypeStruct((keep, D_MODEL), F32))
            out_specs.append(tail_spec)
    return pl.pallas_call(
        functools.partial(_proj_kernel, scales=tuple(scales), tails=tuple(tails),
                          layouts=tuple(layouts), tail_rows=tail_rows,
                          first_tail_step=first_tail_step),
        grid=(n,),
        in_specs=[row_spec] + [w.spec() for w in ws],
        out_specs=out_specs,
        out_shape=out_shape,
        compiler_params=_params(),
        name="proj",
    )(x, *[w.array for w in ws])


def _rel_bias_rows(rvec_ref, h, rows, valid):
    row = jnp.broadcast_to(rvec_ref[h:h + 1, :], (rows, REL_RING))
    t = pltpu.roll(row, 0, 1, stride=1, stride_axis=0)
    return jnp.where(valid, t * LOG2E, MASK_VALUE)


def _softmax_pv(s, v_parts):
    m = jnp.max(s, axis=-1, keepdims=True)
    e = jnp.exp2(s - m)
    l = jnp.sum(e, axis=-1, keepdims=True)
    eb = e.astype(BF16)
    o = None
    off = 0
    for v in v_parts:
        part = _dot(eb[:, off:off + v.shape[0]], v)
        o = part if o is None else o + part
        off += v.shape[0]
    return o / l


def _attn_prompt_kernel(q_ref, kp_ref, kc_ref, vp_ref, vc_ref, rvec_ref, o_ref, bias_ref):
    step = pl.program_id(0)
    low = lax.broadcasted_iota(jnp.int32, (Q_BLOCK, PAIR_W), 1) < HEAD_DIM
    ones = jnp.ones((2 * SUBLANES_V7X, K_BLOCK), BF16)

    def build_bias(first_key):
        qi = lax.broadcasted_iota(jnp.int32, (Q_BLOCK, REL_RING), 0)
        kj = lax.broadcasted_iota(jnp.int32, (Q_BLOCK, REL_RING), 1)
        start = (qi >> 6) << 6
        valid = (kj >= start) & (kj < start + (BAND_PAST + CHUNK)) & (kj >= first_key)
        for h in range(N_HEADS):
            bias_ref[:, h * Q_BLOCK:(h + 1) * Q_BLOCK] = (
                _rel_bias_rows(rvec_ref, h, Q_BLOCK, valid)[:, :K_BLOCK].T)

    for blk in range(ATT_STEP // Q_BLOCK):
        past_lo = blk * Q_BLOCK
        cur_hi = (blk + 1) * Q_BLOCK
        rebuild = (step == 0) if blk > 0 else (step <= 1)
        pl.when(rebuild)(functools.partial(
            build_bias, jnp.where(step == 0, BAND_PAST - past_lo, 0)))

        q_rows = slice(blk * Q_BLOCK, (blk + 1) * Q_BLOCK)

        def scores(p):
            qb = q_ref[p, q_rows, :]
            qs = jnp.concatenate([jnp.where(low, qb, jnp.zeros_like(qb)),
                                  jnp.where(low, jnp.zeros_like(qb), qb)], axis=0)
            kwin = jnp.concatenate([kp_ref[p, past_lo:, :], kc_ref[p, :cur_hi, :]], axis=0)
            return _dot_nt(kwin, qs) + bias_ref[:, 2 * p * Q_BLOCK:2 * (p + 1) * Q_BLOCK]

        def probs(st):
            m = jnp.max(st, axis=0, keepdims=True)
            return jnp.exp2(st - m).astype(BF16)

        def weighted_values(p, eb):
            halves = []
            for hh in range(2):
                v_rows = slice((2 * p + hh) * HEAD_DIM, (2 * p + hh + 1) * HEAD_DIM)
                vwin = jnp.concatenate([vp_ref[v_rows, past_lo:], vc_ref[v_rows, :cur_hi]], axis=1)
                ot = _dot(jnp.concatenate([vwin, ones], axis=0),
                          eb[:, hh * Q_BLOCK:(hh + 1) * Q_BLOCK])
                halves.append(ot[:HEAD_DIM] / ot[HEAD_DIM:HEAD_DIM + 1])
            o_ref[p, q_rows, :] = jnp.concatenate(halves, axis=0).T.astype(BF16)

        st_next = scores(0)
        for p in range(N_PAIRS):
            st = st_next
            if p + 1 < N_PAIRS:
                st_next = scores(p + 1)
            weighted_values(p, probs(st))


def _attn_prompt_call(q, k, vt, rvec):
    s = q.shape[1]
    assert s % ATT_STEP == 0 and ATT_STEP >= BAND_PAST
    n = s // ATT_STEP
    cur = pl.BlockSpec((N_PAIRS, ATT_STEP, PAIR_W), lambda i: (0, i, 0))
    prev = pl.BlockSpec((N_PAIRS, ATT_STEP, PAIR_W), lambda i: (0, jnp.maximum(i - 1, 0), 0))
    cur_t = pl.BlockSpec((D_MODEL, ATT_STEP), lambda i: (0, i))
    prev_t = pl.BlockSpec((D_MODEL, ATT_STEP), lambda i: (0, jnp.maximum(i - 1, 0)))
    return pl.pallas_call(
        _attn_prompt_kernel,
        grid=(n,),
        in_specs=[cur, prev, cur, prev_t, cur_t, _const_spec(rvec.shape)],
        out_specs=cur,
        out_shape=jax.ShapeDtypeStruct((N_PAIRS, s, PAIR_W), BF16),
        scratch_shapes=[pltpu.VMEM((K_BLOCK, N_HEADS * Q_BLOCK), F32)],
        compiler_params=_params(),
        name="attn_prompt",
    )(q, k, k, vt, vt, rvec)


def _attn_decode_kernel(q_ref, kn_ref, vn_ref, kc_ref, vc_ref, rvec_ref, o_ref, bias_ref):
    t = q_ref.shape[0]
    w = kc_ref.shape[1]
    kpad = bias_ref.shape[1]

    @pl.when(pl.program_id(0) == 0)
    def _():
        kj = lax.broadcasted_iota(jnp.int32, (t, REL_RING), 1)
        valid = kj < w + t
        for h in range(N_HEADS):
            bias_ref[h * t:(h + 1) * t, :] = _rel_bias_rows(rvec_ref, h, t, valid)[:, :kpad]

    q = q_ref[...]
    head_of_lane = lax.broadcasted_iota(jnp.int32, (t, D_MODEL), 1) >> 6
    qs = jnp.concatenate(
        [jnp.where(head_of_lane == h, q, jnp.zeros_like(q)) for h in range(N_HEADS)], axis=0)
    zeros = jnp.zeros((kpad - w - t, D_MODEL), BF16)
    kk = jnp.concatenate([kc_ref[0].astype(BF16), kn_ref[...], zeros], axis=0)
    vv = jnp.concatenate([vc_ref[0].astype(BF16), vn_ref[...], zeros], axis=0)
    o = _softmax_pv(_dot_nt(qs, kk) + bias_ref[...], (vv,))
    out = jnp.zeros((t, D_MODEL), F32)
    for h in range(N_HEADS):
        out = out + jnp.where(head_of_lane == h, o[h * t:(h + 1) * t, :], 0.0)
    o_ref[...] = out.astype(BF16)


def _attn_decode_call(q, kn, vn, cache_k, cache_v, rvec, *, t):
    nb, w = cache_k.shape[0], cache_k.shape[1]
    assert w == BAND_PAST
    kpad = -(-(w + t) // LANES_V7X) * LANES_V7X
    assert kpad + t <= REL_RING
    new_spec = pl.BlockSpec((t, D_MODEL), lambda i: (i, 0))
    cache_spec = pl.BlockSpec((1, w, D_MODEL), lambda i: (i, 0, 0))
    return pl.pallas_call(
        _attn_decode_kernel,
        grid=(nb,),
        in_specs=[new_spec, new_spec, new_spec, cache_spec, cache_spec, _const_spec(rvec.shape)],
        out_specs=new_spec,
        out_shape=jax.ShapeDtypeStruct((nb * t, D_MODEL), BF16),
        scratch_shapes=[pltpu.VMEM((N_HEADS * t, kpad), F32)],
        compiler_params=_params(),
        name="attn_decode",
    )(q, kn, vn, cache_k, cache_v, rvec)


def _rel_vector(table):
    m = jnp.arange(REL_RING)
    d = jnp.where(m < K_BLOCK, m, m - REL_RING)
    idx = jnp.clip(BAND_PAST - d, -REL_MAX, REL_MAX) + REL_MAX
    return table[idx].T.astype(F32)


def _trunk(x, conv_prev, caches, wts, *, tm, n_sub, proj_tm, keep):
    s = x.shape[0]
    ln = lambda l: (wts["ln_mix_g"][l][None], wts["ln_mix_b"][l][None],
                    wts["ln_ffn_g"][l][None], wts["ln_ffn_b"][l][None])
    stream_len = 0 if conv_prev is None else s // conv_prev.shape[1]
    prompt = caches is None
    ztails = []
    mlp = lambda l: (_Weight(wts["w_up"], l), _Weight(wts["w_down"], l))
    for l in range(N_A):
        mix = (_Weight(wts["w_in_a"], l), _Weight(wts["conv_w_a"], l), _Weight(wts["w_out_a"], l))
        if stream_len:
            p2 = jnp.pad(conv_prev[l], ((0, 0), (0, stream_len - 2), (0, 0))).reshape(s, D_MODEL)
            p1 = jnp.pad(conv_prev[l][:, 1:], ((0, 0), (0, stream_len - 1), (0, 0))).reshape(s, D_MODEL)
            mix = (p1, p2) + mix
        x, zt = _layer_call(x, mix, ln(l), *mlp(l),
                            conv=True, stream_len=stream_len, pairs=False, tm=tm, n_sub=n_sub)
        ztails.append(zt)

    q_scale = HEAD_DIM ** -0.5 * LOG2E
    qk_layout = "pairs" if prompt else "rows"
    w_v = _Weight(wts["w_v_t"]) if prompt else _Weight(wts["w_v"])
    q, k, kf, v, vf = _proj_call(
        x, (_Weight(wts["w_q_b"], 0), _Weight(wts["w_k"]), w_v), (q_scale, 1.0, 1.0),
        (False, True, True), (qk_layout, qk_layout, "cols" if prompt else "rows"),
        tm=proj_tm, keep=keep)
    for j in range(DEPTH - N_A):
        l = N_A + j
        if j > 0:
            (q,) = _proj_call(x, (_Weight(wts["w_q_b"], j),), (q_scale,), (False,), (qk_layout,),
                              tm=proj_tm, keep=keep)
        rvec = _rel_vector(wts["rel_bias_b"][j])
        if prompt:
            a = _attn_prompt_call(q, k, v, rvec)
        else:
            a = _attn_decode_call(q, k, v, caches[0], caches[1], rvec, t=s // caches[0].shape[0])
        x, = _layer_call(x, (a, _Weight(wts["w_o_b"], j)), ln(l), *mlp(l),
                         conv=False, stream_len=0, pairs=prompt, tm=tm, n_sub=n_sub)
    return x, ztails, kf, vf


def kernel(x_prompt, x_sample, cache_conv, cache_k, cache_v, ln_mix_g, ln_mix_b, ln_ffn_g, ln_ffn_b,
           w_up, w_down, w_in_a, conv_w_a, w_out_a, w_k, w_v, w_q_b, w_o_b, rel_bias_b):
    assert x_prompt.shape[0] == 1
    wts = dict(
        ln_mix_g=ln_mix_g, ln_mix_b=ln_mix_b, ln_ffn_g=ln_ffn_g, ln_ffn_b=ln_ffn_b,
        w_up=w_up.astype(BF16), w_down=w_down.astype(BF16), w_in_a=w_in_a.astype(BF16),
        conv_w_a=conv_w_a, w_out_a=w_out_a.astype(BF16), w_k=w_k.astype(BF16), w_v=w_v.astype(BF16),
        w_q_b=w_q_b.astype(BF16), w_o_b=w_o_b.astype(BF16), rel_bias_b=rel_bias_b,
        w_v_t=w_v.T.astype(BF16))

    seq = x_prompt.shape[1]
    keep = min(BAND_PAST, seq)
    yp, ztp, kfp, vfp = _trunk(x_prompt[0], None, None, wts, tm=512, n_sub=2, proj_tm=1024, keep=keep)
    conv_prompt = jnp.stack([zt[SUBLANES_V7X - 2:] for zt in ztp])[:, None]
    k_prompt = kfp.reshape(1, keep, N_HEADS, HEAD_DIM)
    v_prompt = vfp.reshape(1, keep, N_HEADS, HEAD_DIM)

    nb, t = x_sample.shape[0], x_sample.shape[1]
    w = cache_k.shape[1]
    ns = nb * t
    ys, zts, kfs, vfs = _trunk(
        x_sample.reshape(ns, D_MODEL), cache_conv,
        (cache_k.reshape(nb, w, D_MODEL), cache_v.reshape(nb, w, D_MODEL)),
        wts, tm=ns, n_sub=1, proj_tm=ns, keep=ns)
    conv_sample = jnp.stack([zt.reshape(nb, t, D_MODEL)[:, t - 2:] for zt in zts])
    k_sample = kfs.reshape(nb, t, N_HEADS, HEAD_DIM)
    v_sample = vfs.reshape(nb, t, N_HEADS, HEAD_DIM)
    return (yp[None], ys.reshape(nb, t, D_MODEL), conv_prompt, k_prompt, v_prompt,
            conv_sample, k_sample, v_sample)
```

```python
import dataclasses
import functools
import math

import jax
import jax.numpy as jnp
from jax import lax
from jax.experimental import pallas as pl
from jax.experimental.pallas import tpu as pltpu

D_MODEL = 1024
N_HEADS = 16
HEAD_DIM = 64
DEPTH = 4
N_A = 2
CHUNK = 64
N_PREV_CHUNKS = 8
BAND_PAST = N_PREV_CHUNKS * CHUNK
REL_MAX = 128
ALPHA = (2.0 * DEPTH) ** 0.25
LN_EPS = 1e-5
LOG2E = math.log2(math.e)

LANES_V7X = 128
SUBLANES_V7X = 8
VMEM_BYTES_V7X = 64 * 1024 * 1024
VMEM_LIMIT_BYTES = VMEM_BYTES_V7X - 8 * 1024 * 1024

PAIR_W = LANES_V7X
N_PAIRS = D_MODEL // PAIR_W
Q_BLOCK = 4 * CHUNK
K_BLOCK = BAND_PAST + Q_BLOCK
ATT_STEP = 2 * Q_BLOCK
REL_RING = 1024
MASK_VALUE = -1e30

BF16 = jnp.bfloat16
F32 = jnp.float32


def _const_spec(shape):
    zeros = (0,) * len(shape)
    return pl.BlockSpec(shape, lambda i: zeros, pipeline_mode=pl.Buffered(1))


@dataclasses.dataclass(frozen=True)
class _Weight:
    array: jax.Array
    layer: int | None = None

    @property
    def shape(self):
        return self.array.shape if self.layer is None else self.array.shape[1:]

    def spec(self):
        if self.layer is None:
            return _const_spec(self.array.shape)
        index = (self.layer,) + (0,) * len(self.shape)
        return pl.BlockSpec((None,) + self.shape, lambda i: index, pipeline_mode=pl.Buffered(1))


def _params():
    return pltpu.CompilerParams(dimension_semantics=("arbitrary",), vmem_limit_bytes=VMEM_LIMIT_BYTES)


def _dot(a, b):
    return jnp.dot(a, b, preferred_element_type=F32)


def _dot_nt(a, b):
    return lax.dot_general(a, b, (((1,), (1,)), ((), ())), preferred_element_type=F32)


def _layer_norm(x, g, b):
    mu = jnp.mean(x, axis=-1, keepdims=True)
    xc = x - mu
    var = jnp.mean(xc * xc, axis=-1, keepdims=True)
    return xc * lax.rsqrt(var + LN_EPS) * g + b


def _store_pairs(ref, val):
    for p in range(N_PAIRS):
        ref[p] = val[:, p * PAIR_W:(p + 1) * PAIR_W].astype(BF16)


@dataclasses.dataclass(frozen=True)
class _LayerCfg:
    conv: bool
    stream_len: int
    pairs: bool
    tm: int
    n_sub: int
    q_scale: float

    @property
    def streams(self):
        return self.stream_len > 0


def _layer_kernel(*refs, cfg):
    it = iter(refs)
    x_ref = next(it)
    if cfg.conv:
        if cfg.streams:
            p1_ref, p2_ref = next(it), next(it)
        w_in_ref, cw_ref, w_mix_ref = next(it), next(it), next(it)
    else:
        a_ref, w_mix_ref = next(it), next(it)
    g1_ref, b1_ref, g2_ref, b2_ref = next(it), next(it), next(it), next(it)
    w_up_ref, w_down_ref = next(it), next(it)
    if cfg.q_scale:
        w_q_ref = next(it)
    y_ref = next(it)
    if cfg.q_scale:
        q_ref = next(it)
    if cfg.conv:
        ztail_ref = next(it)
        if not cfg.streams:
            carry_ref = next(it)

    if cfg.conv and not cfg.streams:
        @pl.when(pl.program_id(0) == 0)
        def _():
            carry_ref[...] = jnp.zeros_like(carry_ref)
        prev = carry_ref[...]

    sm = cfg.tm // cfg.n_sub
    subs = range(cfg.n_sub)
    rows = [slice(sub * sm, (sub + 1) * sm) for sub in subs]
    x = [x_ref[r, :] for r in rows]
    if cfg.conv:
        bch = [_dot(x[i].astype(BF16), w_in_ref[...]) for i in subs]
        row = lax.broadcasted_iota(jnp.int32, (sm, D_MODEL), 0)
        cw = cw_ref[...]
        mix_in = []
        for i in subs:
            b_gate = bch[i][:, :D_MODEL]
            z = bch[i][:, D_MODEL:2 * D_MODEL] * bch[i][:, 2 * D_MODEL:]
            r1 = pltpu.roll(z, 1, 0)
            r2 = pltpu.roll(z, 2, 0)
            if cfg.streams:
                rm = row & (cfg.stream_len - 1)
                zm1 = jnp.where(rm == 0, p1_ref[rows[i], :], r1)
                zm2 = jnp.where(rm < 2, p2_ref[rows[i], :], r2)
                ztail_ref[rows[i], :] = z
            else:
                pm1 = prev[SUBLANES_V7X - 1:SUBLANES_V7X, :]
                pm2 = prev[SUBLANES_V7X - 2:SUBLANES_V7X - 1, :]
                zm1 = jnp.where(row == 0, pm1, r1)
                zm2 = jnp.where(row == 0, pm2, jnp.where(row == 1, pm1, r2))
                prev = z[sm - SUBLANES_V7X:, :]
            yc = cw[0:1, :] * zm2 + cw[1:2, :] * zm1 + cw[2:3, :] * z
            mix_in.append((b_gate * yc).astype(BF16))
        if not cfg.streams:
            carry_ref[...] = prev
            ztail_ref[...] = prev
    elif cfg.pairs:
        mix_in = [jnp.concatenate([a_ref[p, r, :] for p in range(N_PAIRS)], axis=-1) for r in rows]
    else:
        mix_in = [a_ref[r, :] for r in rows]
    h = [_dot(mix_in[i], w_mix_ref[...]) for i in subs]
    x1 = [_layer_norm(ALPHA * x[i] + h[i], g1_ref[...], b1_ref[...]) for i in subs]
    u = [_dot(x1[i].astype(BF16), w_up_ref[...]) for i in subs]
    u = [jnp.maximum(u[i], 0.0) for i in subs]
    m = [_dot((u[i] * u[i]).astype(BF16), w_down_ref[...]) for i in subs]
    y = [_layer_norm(ALPHA * x1[i] + m[i], g2_ref[...], b2_ref[...]) for i in subs]
    for i in subs:
        y_ref[rows[i], :] = y[i]
    if cfg.q_scale:
        q = [_dot(y[i].astype(BF16), w_q_ref[...]) * cfg.q_scale for i in subs]
        for i in subs:
            for p in range(N_PAIRS):
                q_ref[p, rows[i], :] = q[i][:, p * PAIR_W:(p + 1) * PAIR_W].astype(BF16)


def _layer_call(x, mix_args, ln, w_up, w_down, *, conv, stream_len, pairs, tm, n_sub,
                w_q=None, q_scale=0.0):
    s = x.shape[0]
    assert s % tm == 0 and tm % n_sub == 0
    assert stream_len & (stream_len - 1) == 0 and (stream_len == 0 or (tm // n_sub) % stream_len == 0)
    assert (w_q is None) == (q_scale == 0.0)
    n = s // tm
    cfg = _LayerCfg(conv=conv, stream_len=stream_len, pairs=pairs, tm=tm, n_sub=n_sub, q_scale=q_scale)
    streams = cfg.streams
    row_spec = pl.BlockSpec((tm, D_MODEL), lambda i: (i, 0))
    pair_spec = pl.BlockSpec((N_PAIRS, tm, PAIR_W), lambda i: (0, i, 0))
    in_specs = [row_spec]
    args = [x]
    if conv:
        if streams:
            p1, p2, w_in, cw, w_out = mix_args
            in_specs += [row_spec, row_spec]
            args += [p1, p2]
        else:
            w_in, cw, w_out = mix_args
        in_specs += [w_in.spec(), cw.spec(), w_out.spec()]
        args += [w_in.array, cw.array, w_out.array]
    else:
        a, w_o = mix_args
        in_specs += [pair_spec if pairs else row_spec, w_o.spec()]
        args += [a, w_o.array]
    in_specs += [_const_spec((1, D_MODEL))] * 4 + [w_up.spec(), w_down.spec()]
    args += list(ln) + [w_up.array, w_down.array]

    out_shape = [jax.ShapeDtypeStruct((s, D_MODEL), F32)]
    out_specs = [row_spec]
    if w_q is not None:
        in_specs.append(w_q.spec())
        args.append(w_q.array)
        out_shape.append(jax.ShapeDtypeStruct((N_PAIRS, s, PAIR_W), BF16))
        out_specs.append(pair_spec)
    scratch = []
    if conv:
        if streams:
            out_shape.append(jax.ShapeDtypeStruct((s, D_MODEL), F32))
            out_specs.append(row_spec)
        else:
            out_shape.append(jax.ShapeDtypeStruct((SUBLANES_V7X, D_MODEL), F32))
            out_specs.append(pl.BlockSpec((SUBLANES_V7X, D_MODEL), lambda i: (0, 0)))
            scratch.append(pltpu.VMEM((SUBLANES_V7X, D_MODEL), F32))
    return pl.pallas_call(
        functools.partial(_layer_kernel, cfg=cfg),
        grid=(n,),
        in_specs=in_specs,
        out_specs=out_specs,
        out_shape=out_shape,
        scratch_shapes=scratch,
        compiler_params=_params(),
        name="layer_conv" if conv else "layer_attn",
    )(*args)


def _proj_kernel(*refs, scales, tails, tail_rows, pairs):
    n = len(scales)
    x_ref = refs[0]
    w_refs = refs[1:1 + n]
    out_refs = list(refs[1 + n:])
    xb = x_ref[...].astype(BF16)
    tm = xb.shape[0]
    for j in range(n):
        pj = _dot(xb, w_refs[j][...])
        if scales[j] != 1.0:
            pj = pj * scales[j]
        if pairs:
            _store_pairs(out_refs.pop(0), pj)
        else:
            out_refs.pop(0)[...] = pj.astype(BF16)
        if tails[j]:
            out_refs.pop(0)[...] = pj[tm - tail_rows:, :]


def _proj_call(x, ws, scales, tails, *, tm, keep, pairs):
    s = x.shape[0]
    assert s % tm == 0
    n = s // tm
    tail_rows = min(tm, keep)
    assert keep % tail_rows == 0
    n_tail_blocks = keep // tail_rows
    row_spec = pl.BlockSpec((tm, D_MODEL), lambda i: (i, 0))
    pair_spec = pl.BlockSpec((N_PAIRS, tm, PAIR_W), lambda i: (0, i, 0))
    tail_spec = pl.BlockSpec((tail_rows, D_MODEL),
                             lambda i: (jnp.maximum(i - (n - n_tail_blocks), 0), 0))
    out_shape, out_specs = [], []
    for t in tails:
        if pairs:
            out_shape.append(jax.ShapeDtypeStruct((N_PAIRS, s, PAIR_W), BF16))
            out_specs.append(pair_spec)
        else:
            out_shape.append(jax.ShapeDtypeStruct((s, D_MODEL), BF16))
            out_specs.append(row_spec)
        if t:
            out_shape.append(jax.ShapeDtypeStruct((keep, D_MODEL), F32))
            out_specs.append(tail_spec)
    return pl.pallas_call(
        functools.partial(_proj_kernel, scales=tuple(scales), tails=tuple(tails),
                          tail_rows=tail_rows, pairs=pairs),
        grid=(n,),
        in_specs=[row_spec] + [w.spec() for w in ws],
        out_specs=out_specs,
        out_shape=out_shape,
        compiler_params=_params(),
        name="proj",
    )(x, *[w.array for w in ws])


def _rel_bias_rows(rvec_ref, h, rows, valid):
    row = jnp.broadcast_to(rvec_ref[h:h + 1, :], (rows, REL_RING))
    t = pltpu.roll(row, 0, 1, stride=1, stride_axis=0)
    return jnp.where(valid, t * LOG2E, MASK_VALUE)


def _softmax_weights(s):
    m = jnp.max(s, axis=-1, keepdims=True)
    e = jnp.exp2(s - m)
    return e.astype(BF16), jnp.sum(e, axis=-1, keepdims=True)


def _weighted_values(eb, l, v_parts):
    o = None
    off = 0
    for v in v_parts:
        part = _dot(eb[:, off:off + v.shape[0]], v)
        o = part if o is None else o + part
        off += v.shape[0]
    return o / l


def _attn_prompt_kernel(q_ref, kp_ref, kc_ref, vp_ref, vc_ref, rvec_ref, o_ref, bias_ref):
    step = pl.program_id(0)
    low = lax.broadcasted_iota(jnp.int32, (Q_BLOCK, PAIR_W), 1) < HEAD_DIM

    def build_bias(first_key):
        qi = lax.broadcasted_iota(jnp.int32, (Q_BLOCK, REL_RING), 0)
        kj = lax.broadcasted_iota(jnp.int32, (Q_BLOCK, REL_RING), 1)
        start = (qi >> 6) << 6
        valid = (kj >= start) & (kj < start + (BAND_PAST + CHUNK)) & (kj >= first_key)
        for h in range(N_HEADS):
            bias_ref[h * Q_BLOCK:(h + 1) * Q_BLOCK, :] = (
                _rel_bias_rows(rvec_ref, h, Q_BLOCK, valid)[:, :K_BLOCK])

    for blk in range(ATT_STEP // Q_BLOCK):
        past_lo = blk * Q_BLOCK
        cur_hi = (blk + 1) * Q_BLOCK
        rebuild = (step == 0) if blk > 0 else (step <= 1)
        pl.when(rebuild)(functools.partial(
            build_bias, jnp.where(step == 0, BAND_PAST - past_lo, 0)))

        q_rows = slice(blk * Q_BLOCK, (blk + 1) * Q_BLOCK)

        def scores(p):
            qb = q_ref[p, q_rows, :]
            qs = jnp.concatenate([jnp.where(low, qb, jnp.zeros_like(qb)),
                                  jnp.where(low, jnp.zeros_like(qb), qb)], axis=0)
            s = jnp.concatenate([_dot_nt(qs, kp_ref[p, past_lo:, :]),
                                 _dot_nt(qs, kc_ref[p, :cur_hi, :])], axis=-1)
            return s + bias_ref[2 * p * Q_BLOCK:2 * (p + 1) * Q_BLOCK, :]

        def finish(p, s):
            o = _weighted_values(*_softmax_weights(s), (vp_ref[p, past_lo:, :], vc_ref[p, :cur_hi, :]))
            o_ref[p, q_rows, :] = jnp.where(low, o[:Q_BLOCK], o[Q_BLOCK:]).astype(BF16)

        s_next = scores(0)
        for p in range(N_PAIRS):
            s_cur = s_next
            if p + 1 < N_PAIRS:
                s_next = scores(p + 1)
            finish(p, s_cur)


def _attn_prompt_call(q, k, v, rvec):
    s = q.shape[1]
    assert s % ATT_STEP == 0 and ATT_STEP >= BAND_PAST
    n = s // ATT_STEP
    cur = pl.BlockSpec((N_PAIRS, ATT_STEP, PAIR_W), lambda i: (0, i, 0))
    prev = pl.BlockSpec((N_PAIRS, ATT_STEP, PAIR_W), lambda i: (0, jnp.maximum(i - 1, 0), 0))
    return pl.pallas_call(
        _attn_prompt_kernel,
        grid=(n,),
        in_specs=[cur, prev, cur, prev, cur, _const_spec(rvec.shape)],
        out_specs=cur,
        out_shape=jax.ShapeDtypeStruct((N_PAIRS, s, PAIR_W), BF16),
        scratch_shapes=[pltpu.VMEM((N_HEADS * Q_BLOCK, K_BLOCK), F32)],
        compiler_params=_params(),
        name="attn_prompt",
    )(q, k, k, v, v, rvec)


def _attn_decode_kernel(q_ref, kn_ref, vn_ref, kc_ref, vc_ref, rvec_ref, o_ref, bias_ref):
    t = q_ref.shape[0]
    w = kc_ref.shape[1]
    kpad = bias_ref.shape[1]

    @pl.when(pl.program_id(0) == 0)
    def _():
        kj = lax.broadcasted_iota(jnp.int32, (t, REL_RING), 1)
        valid = kj < w + t
        for h in range(N_HEADS):
            bias_ref[h * t:(h + 1) * t, :] = _rel_bias_rows(rvec_ref, h, t, valid)[:, :kpad]

    q = q_ref[...]
    head_of_lane = lax.broadcasted_iota(jnp.int32, (t, D_MODEL), 1) >> 6
    qs = jnp.concatenate(
        [jnp.where(head_of_lane == h, q, jnp.zeros_like(q)) for h in range(N_HEADS)], axis=0)
    zeros = jnp.zeros((kpad - w - t, D_MODEL), BF16)
    kk = jnp.concatenate([kc_ref[0].astype(BF16), kn_ref[...], zeros], axis=0)
    vv = jnp.concatenate([vc_ref[0].astype(BF16), vn_ref[...], zeros], axis=0)
    eb, l = _softmax_weights(_dot_nt(qs, kk) + bias_ref[...])
    o = _weighted_values(eb, l, (vv,))
    out = jnp.zeros((t, D_MODEL), F32)
    for h in range(N_HEADS):
        out = out + jnp.where(head_of_lane == h, o[h * t:(h + 1) * t, :], 0.0)
    o_ref[...] = out.astype(BF16)


def _attn_decode_call(q, kn, vn, cache_k, cache_v, rvec, *, t):
    nb, w = cache_k.shape[0], cache_k.shape[1]
    assert w == BAND_PAST
    kpad = -(-(w + t) // LANES_V7X) * LANES_V7X
    assert kpad + t <= REL_RING
    new_spec = pl.BlockSpec((t, D_MODEL), lambda i: (i, 0))
    cache_spec = pl.BlockSpec((1, w, D_MODEL), lambda i: (i, 0, 0))
    return pl.pallas_call(
        _attn_decode_kernel,
        grid=(nb,),
        in_specs=[new_spec, new_spec, new_spec, cache_spec, cache_spec, _const_spec(rvec.shape)],
        out_specs=new_spec,
        out_shape=jax.ShapeDtypeStruct((nb * t, D_MODEL), BF16),
        scratch_shapes=[pltpu.VMEM((N_HEADS * t, kpad), F32)],
        compiler_params=_params(),
        name="attn_decode",
    )(q, kn, vn, cache_k, cache_v, rvec)


def _rel_vector(table):
    m = jnp.arange(REL_RING)
    d = jnp.where(m < K_BLOCK, m, m - REL_RING)
    idx = jnp.clip(BAND_PAST - d, -REL_MAX, REL_MAX) + REL_MAX
    return table[idx].T.astype(F32)


def _trunk(x, conv_prev, caches, wts, *, tm, n_sub, proj_tm, keep):
    s = x.shape[0]
    ln = lambda l: (wts["ln_mix_g"][l][None], wts["ln_mix_b"][l][None],
                    wts["ln_ffn_g"][l][None], wts["ln_ffn_b"][l][None])
    stream_len = 0 if conv_prev is None else s // conv_prev.shape[1]
    pairs = caches is None
    ztails = []
    mlp = lambda l: (_Weight(wts["w_up"], l), _Weight(wts["w_down"], l))
    for l in range(N_A):
        mix = (_Weight(wts["w_in_a"], l), _Weight(wts["conv_w_a"], l), _Weight(wts["w_out_a"], l))
        if stream_len:
            p2 = jnp.pad(conv_prev[l], ((0, 0), (0, stream_len - 2), (0, 0))).reshape(s, D_MODEL)
            p1 = jnp.pad(conv_prev[l][:, 1:], ((0, 0), (0, stream_len - 1), (0, 0))).reshape(s, D_MODEL)
            mix = (p1, p2) + mix
        x, zt = _layer_call(x, mix, ln(l), *mlp(l),
                            conv=True, stream_len=stream_len, pairs=False, tm=tm, n_sub=n_sub)
        ztails.append(zt)

    q_scale = HEAD_DIM ** -0.5 * LOG2E
    q, k, kf, v, vf = _proj_call(
        x, (_Weight(wts["w_q_b"], 0), _Weight(wts["w_k"]), _Weight(wts["w_v"])),
        (q_scale, 1.0, 1.0), (False, True, True), tm=proj_tm, keep=keep, pairs=pairs)
    n_b = DEPTH - N_A
    for j in range(n_b):
        l = N_A + j
        rvec = _rel_vector(wts["rel_bias_b"][j])
        if caches is None:
            a = _attn_prompt_call(q, k, v, rvec)
        else:
            a = _attn_decode_call(q, k, v, caches[0], caches[1], rvec, t=s // caches[0].shape[0])
        fuse_q = pairs and j + 1 < n_b
        outs = _layer_call(x, (a, _Weight(wts["w_o_b"], j)), ln(l), *mlp(l),
                           conv=False, stream_len=0, pairs=pairs, tm=tm, n_sub=n_sub,
                           w_q=_Weight(wts["w_q_b"], j + 1) if fuse_q else None,
                           q_scale=q_scale if fuse_q else 0.0)
        x = outs[0]
        if fuse_q:
            q = outs[1]
        elif j + 1 < n_b:
            (q,) = _proj_call(x, (_Weight(wts["w_q_b"], j + 1),), (q_scale,), (False,),
                              tm=proj_tm, keep=keep, pairs=pairs)
    return x, ztails, kf, vf


def kernel(x_prompt, x_sample, cache_conv, cache_k, cache_v, ln_mix_g, ln_mix_b, ln_ffn_g, ln_ffn_b,
           w_up, w_down, w_in_a, conv_w_a, w_out_a, w_k, w_v, w_q_b, w_o_b, rel_bias_b):
    assert x_prompt.shape[0] == 1
    wts = dict(
        ln_mix_g=ln_mix_g, ln_mix_b=ln_mix_b, ln_ffn_g=ln_ffn_g, ln_ffn_b=ln_ffn_b,
        w_up=w_up.astype(BF16), w_down=w_down.astype(BF16), w_in_a=w_in_a.astype(BF16),
        conv_w_a=conv_w_a, w_out_a=w_out_a.astype(BF16), w_k=w_k.astype(BF16), w_v=w_v.astype(BF16),
        w_q_b=w_q_b.astype(BF16), w_o_b=w_o_b.astype(BF16), rel_bias_b=rel_bias_b)

    seq = x_prompt.shape[1]
    keep = min(BAND_PAST, seq)
    yp, ztp, kfp, vfp = _trunk(x_prompt[0], None, None, wts, tm=512, n_sub=2, proj_tm=1024, keep=keep)
    conv_prompt = jnp.stack([zt[SUBLANES_V7X - 2:] for zt in ztp])[:, None]
    k_prompt = kfp.reshape(1, keep, N_HEADS, HEAD_DIM)
    v_prompt = vfp.reshape(1, keep, N_HEADS, HEAD_DIM)

    nb, t = x_sample.shape[0], x_sample.shape[1]
    w = cache_k.shape[1]
    ns = nb * t
    ys, zts, kfs, vfs = _trunk(
        x_sample.reshape(ns, D_MODEL), cache_conv,
        (cache_k.reshape(nb, w, D_MODEL), cache_v.reshape(nb, w, D_MODEL)),
        wts, tm=ns, n_sub=1, proj_tm=ns, keep=ns)
    conv_sample = jnp.stack([zt.reshape(nb, t, D_MODEL)[:, t - 2:] for zt in zts])
    k_sample = kfs.reshape(nb, t, N_HEADS, HEAD_DIM)
    v_sample = vfs.reshape(nb, t, N_HEADS, HEAD_DIM)
    return (yp[None], ys.reshape(nb, t, D_MODEL), conv_prompt, k_prompt, v_prompt,
            conv_sample, k_sample, v_sample)
```

```python
import dataclasses
import functools
import math

import jax
import jax.numpy as jnp
from jax import lax
from jax.experimental import pallas as pl
from jax.experimental.pallas import tpu as pltpu

D_MODEL = 1024
N_HEADS = 16
HEAD_DIM = 64
DEPTH = 4
N_A = 2
CHUNK = 64
N_PREV_CHUNKS = 8
BAND_PAST = N_PREV_CHUNKS * CHUNK
REL_MAX = 128
ALPHA = (2.0 * DEPTH) ** 0.25
LN_EPS = 1e-5
LOG2E = math.log2(math.e)

LANES_V7X = 128
SUBLANES_V7X = 8
VMEM_BYTES_V7X = 64 * 1024 * 1024
VMEM_LIMIT_BYTES = VMEM_BYTES_V7X - 8 * 1024 * 1024

PAIR_W = LANES_V7X
N_PAIRS = D_MODEL // PAIR_W
Q_BLOCK = 2 * CHUNK
K_BLOCK = BAND_PAST + Q_BLOCK
ATT_STEP = BAND_PAST
REL_RING = 1024
MASK_VALUE = -1e30

BF16 = jnp.bfloat16
F32 = jnp.float32


def _const_spec(shape):
    zeros = (0,) * len(shape)
    return pl.BlockSpec(shape, lambda i: zeros, pipeline_mode=pl.Buffered(1))


@dataclasses.dataclass(frozen=True)
class _Weight:
    array: jax.Array
    layer: int | None = None

    @property
    def shape(self):
        return self.array.shape if self.layer is None else self.array.shape[1:]

    def spec(self):
        if self.layer is None:
            return _const_spec(self.array.shape)
        index = (self.layer,) + (0,) * len(self.shape)
        return pl.BlockSpec((None,) + self.shape, lambda i: index, pipeline_mode=pl.Buffered(1))


def _params():
    return pltpu.CompilerParams(dimension_semantics=("arbitrary",), vmem_limit_bytes=VMEM_LIMIT_BYTES)


def _dot(a, b):
    return jnp.dot(a, b, preferred_element_type=F32)


def _dot_nt(a, b):
    return lax.dot_general(a, b, (((1,), (1,)), ((), ())), preferred_element_type=F32)


def _layer_norm(x, g, b):
    mu = jnp.mean(x, axis=-1, keepdims=True)
    xc = x - mu
    var = jnp.mean(xc * xc, axis=-1, keepdims=True)
    return xc * lax.rsqrt(var + LN_EPS) * g + b


def _store_pairs(ref, val):
    for p in range(N_PAIRS):
        ref[p] = val[:, p * PAIR_W:(p + 1) * PAIR_W].astype(BF16)


@dataclasses.dataclass(frozen=True)
class _LayerCfg:
    conv: bool
    stream_len: int
    pairs: bool
    tm: int
    n_sub: int
    q_scale: float

    @property
    def streams(self):
        return self.stream_len > 0


def _layer_kernel(*refs, cfg):
    it = iter(refs)
    x_ref = next(it)
    if cfg.conv:
        if cfg.streams:
            p1_ref, p2_ref = next(it), next(it)
        w_in_ref, cw_ref, w_mix_ref = next(it), next(it), next(it)
    else:
        a_ref, w_mix_ref = next(it), next(it)
    g1_ref, b1_ref, g2_ref, b2_ref = next(it), next(it), next(it), next(it)
    w_up_ref, w_down_ref = next(it), next(it)
    if cfg.q_scale:
        w_q_ref = next(it)
    y_ref = next(it)
    if cfg.q_scale:
        q_ref = next(it)
    if cfg.conv:
        ztail_ref = next(it)
        if not cfg.streams:
            carry_ref = next(it)

    if cfg.conv and not cfg.streams:
        @pl.when(pl.program_id(0) == 0)
        def _():
            carry_ref[...] = jnp.zeros_like(carry_ref)
        prev = carry_ref[...]

    sm = cfg.tm // cfg.n_sub
    subs = range(cfg.n_sub)
    rows = [slice(sub * sm, (sub + 1) * sm) for sub in subs]
    x = [x_ref[r, :] for r in rows]
    if cfg.conv:
        bch = [_dot(x[i].astype(BF16), w_in_ref[...]) for i in subs]
        row = lax.broadcasted_iota(jnp.int32, (sm, D_MODEL), 0)
        cw = cw_ref[...]
        mix_in = []
        for i in subs:
            b_gate = bch[i][:, :D_MODEL]
            z = bch[i][:, D_MODEL:2 * D_MODEL] * bch[i][:, 2 * D_MODEL:]
            r1 = pltpu.roll(z, 1, 0)
            r2 = pltpu.roll(z, 2, 0)
            if cfg.streams:
                rm = row & (cfg.stream_len - 1)
                zm1 = jnp.where(rm == 0, p1_ref[rows[i], :], r1)
                zm2 = jnp.where(rm < 2, p2_ref[rows[i], :], r2)
                ztail_ref[rows[i], :] = z
            else:
                pm1 = prev[SUBLANES_V7X - 1:SUBLANES_V7X, :]
                pm2 = prev[SUBLANES_V7X - 2:SUBLANES_V7X - 1, :]
                zm1 = jnp.where(row == 0, pm1, r1)
                zm2 = jnp.where(row == 0, pm2, jnp.where(row == 1, pm1, r2))
                prev = z[sm - SUBLANES_V7X:, :]
            yc = cw[0:1, :] * zm2 + cw[1:2, :] * zm1 + cw[2:3, :] * z
            mix_in.append((b_gate * yc).astype(BF16))
        if not cfg.streams:
            carry_ref[...] = prev
            ztail_ref[...] = prev
    elif cfg.pairs:
        mix_in = [jnp.concatenate([a_ref[p, r, :] for p in range(N_PAIRS)], axis=-1) for r in rows]
    else:
        mix_in = [a_ref[r, :] for r in rows]
    h = [_dot(mix_in[i], w_mix_ref[...]) for i in subs]
    x1 = [_layer_norm(ALPHA * x[i] + h[i], g1_ref[...], b1_ref[...]) for i in subs]
    u = [_dot(x1[i].astype(BF16), w_up_ref[...]) for i in subs]
    u = [jnp.maximum(u[i], 0.0) for i in subs]
    m = [_dot((u[i] * u[i]).astype(BF16), w_down_ref[...]) for i in subs]
    y = [_layer_norm(ALPHA * x1[i] + m[i], g2_ref[...], b2_ref[...]) for i in subs]
    for i in subs:
        y_ref[rows[i], :] = y[i]
    if cfg.q_scale:
        q = [_dot(y[i].astype(BF16), w_q_ref[...]) * cfg.q_scale for i in subs]
        for i in subs:
            for p in range(N_PAIRS):
                q_ref[p, rows[i], :] = q[i][:, p * PAIR_W:(p + 1) * PAIR_W].astype(BF16)


def _layer_call(x, mix_args, ln, w_up, w_down, *, conv, stream_len, pairs, tm, n_sub,
                w_q=None, q_scale=0.0):
    s = x.shape[0]
    assert s % tm == 0 and tm % n_sub == 0
    assert stream_len & (stream_len - 1) == 0 and (stream_len == 0 or (tm // n_sub) % stream_len == 0)
    assert (w_q is None) == (q_scale == 0.0)
    n = s // tm
    cfg = _LayerCfg(conv=conv, stream_len=stream_len, pairs=pairs, tm=tm, n_sub=n_sub, q_scale=q_scale)
    streams = cfg.streams
    row_spec = pl.BlockSpec((tm, D_MODEL), lambda i: (i, 0))
    pair_spec = pl.BlockSpec((N_PAIRS, tm, PAIR_W), lambda i: (0, i, 0))
    in_specs = [row_spec]
    args = [x]
    if conv:
        if streams:
            p1, p2, w_in, cw, w_out = mix_args
            in_specs += [row_spec, row_spec]
            args += [p1, p2]
        else:
            w_in, cw, w_out = mix_args
        in_specs += [w_in.spec(), cw.spec(), w_out.spec()]
        args += [w_in.array, cw.array, w_out.array]
    else:
        a, w_o = mix_args
        in_specs += [pair_spec if pairs else row_spec, w_o.spec()]
        args += [a, w_o.array]
    in_specs += [_const_spec((1, D_MODEL))] * 4 + [w_up.spec(), w_down.spec()]
    args += list(ln) + [w_up.array, w_down.array]

    out_shape = [jax.ShapeDtypeStruct((s, D_MODEL), F32)]
    out_specs = [row_spec]
    if w_q is not None:
        in_specs.append(w_q.spec())
        args.append(w_q.array)
        out_shape.append(jax.ShapeDtypeStruct((N_PAIRS, s, PAIR_W), BF16))
        out_specs.append(pair_spec)
    scratch = []
    if conv:
        if streams:
            out_shape.append(jax.ShapeDtypeStruct((s, D_MODEL), F32))
            out_specs.append(row_spec)
        else:
            out_shape.append(jax.ShapeDtypeStruct((SUBLANES_V7X, D_MODEL), F32))
            out_specs.append(pl.BlockSpec((SUBLANES_V7X, D_MODEL), lambda i: (0, 0)))
            scratch.append(pltpu.VMEM((SUBLANES_V7X, D_MODEL), F32))
    return pl.pallas_call(
        functools.partial(_layer_kernel, cfg=cfg),
        grid=(n,),
        in_specs=in_specs,
        out_specs=out_specs,
        out_shape=out_shape,
        scratch_shapes=scratch,
        compiler_params=_params(),
        name="layer_conv" if conv else "layer_attn",
    )(*args)


def _proj_kernel(*refs, scales, tails, tail_rows, pairs):
    n = len(scales)
    x_ref = refs[0]
    w_refs = refs[1:1 + n]
    out_refs = list(refs[1 + n:])
    xb = x_ref[...].astype(BF16)
    tm = xb.shape[0]
    for j in range(n):
        pj = _dot(xb, w_refs[j][...])
        if scales[j] != 1.0:
            pj = pj * scales[j]
        if pairs:
            _store_pairs(out_refs.pop(0), pj)
        else:
            out_refs.pop(0)[...] = pj.astype(BF16)
        if tails[j]:
            out_refs.pop(0)[...] = pj[tm - tail_rows:, :]


def _proj_call(x, ws, scales, tails, *, tm, keep, pairs):
    s = x.shape[0]
    assert s % tm == 0
    n = s // tm
    tail_rows = min(tm, keep)
    assert keep % tail_rows == 0
    n_tail_blocks = keep // tail_rows
    row_spec = pl.BlockSpec((tm, D_MODEL), lambda i: (i, 0))
    pair_spec = pl.BlockSpec((N_PAIRS, tm, PAIR_W), lambda i: (0, i, 0))
    tail_spec = pl.BlockSpec((tail_rows, D_MODEL),
                             lambda i: (jnp.maximum(i - (n - n_tail_blocks), 0), 0))
    out_shape, out_specs = [], []
    for t in tails:
        if pairs:
            out_shape.append(jax.ShapeDtypeStruct((N_PAIRS, s, PAIR_W), BF16))
            out_specs.append(pair_spec)
        else:
            out_shape.append(jax.ShapeDtypeStruct((s, D_MODEL), BF16))
            out_specs.append(row_spec)
        if t:
            out_shape.append(jax.ShapeDtypeStruct((keep, D_MODEL), F32))
            out_specs.append(tail_spec)
    return pl.pallas_call(
        functools.partial(_proj_kernel, scales=tuple(scales), tails=tuple(tails),
                          tail_rows=tail_rows, pairs=pairs),
        grid=(n,),
        in_specs=[row_spec] + [w.spec() for w in ws],
        out_specs=out_specs,
        out_shape=out_shape,
        compiler_params=_params(),
        name="proj",
    )(x, *[w.array for w in ws])


def _rel_bias_rows(rvec_ref, h, rows, valid):
    row = jnp.broadcast_to(rvec_ref[h:h + 1, :], (rows, REL_RING))
    t = pltpu.roll(row, 0, 1, stride=1, stride_axis=0)
    return jnp.where(valid, t * LOG2E, MASK_VALUE)


def _softmax_weights(s):
    m = jnp.max(s, axis=-1, keepdims=True)
    e = jnp.exp2(s - m)
    return e.astype(BF16), jnp.sum(e, axis=-1, keepdims=True)


def _weighted_values(eb, l, v_parts):
    o = None
    off = 0
    for v in v_parts:
        part = _dot(eb[:, off:off + v.shape[0]], v)
        o = part if o is None else o + part
        off += v.shape[0]
    return o / l


def _attn_prompt_kernel(q_ref, kp_ref, kc_ref, vp_ref, vc_ref, rvec_ref, o_ref, bias_ref):
    step = pl.program_id(0)
    low = lax.broadcasted_iota(jnp.int32, (Q_BLOCK, PAIR_W), 1) < HEAD_DIM

    def build_bias(first_key):
        qi = lax.broadcasted_iota(jnp.int32, (Q_BLOCK, REL_RING), 0)
        kj = lax.broadcasted_iota(jnp.int32, (Q_BLOCK, REL_RING), 1)
        start = (qi >> 6) << 6
        valid = (kj >= start) & (kj < start + (BAND_PAST + CHUNK)) & (kj >= first_key)
        for h in range(N_HEADS):
            bias_ref[h * Q_BLOCK:(h + 1) * Q_BLOCK, :] = (
                _rel_bias_rows(rvec_ref, h, Q_BLOCK, valid)[:, :K_BLOCK])

    for blk in range(ATT_STEP // Q_BLOCK):
        past_lo = blk * Q_BLOCK
        cur_hi = (blk + 1) * Q_BLOCK
        rebuild = (step == 0) if blk > 0 else (step <= 1)
        pl.when(rebuild)(functools.partial(
            build_bias, jnp.where(step == 0, BAND_PAST - past_lo, 0)))

        q_rows = slice(blk * Q_BLOCK, (blk + 1) * Q_BLOCK)

        def scores(p):
            qb = q_ref[p, q_rows, :]
            qs = jnp.concatenate([jnp.where(low, qb, jnp.zeros_like(qb)),
                                  jnp.where(low, jnp.zeros_like(qb), qb)], axis=0)
            s = jnp.concatenate([_dot_nt(qs, kp_ref[p, past_lo:, :]),
                                 _dot_nt(qs, kc_ref[p, :cur_hi, :])], axis=-1)
            return s + bias_ref[2 * p * Q_BLOCK:2 * (p + 1) * Q_BLOCK, :]

        def finish(p, s):
            o = _weighted_values(*_softmax_weights(s), (vp_ref[p, past_lo:, :], vc_ref[p, :cur_hi, :]))
            o_ref[p, q_rows, :] = jnp.where(low, o[:Q_BLOCK], o[Q_BLOCK:]).astype(BF16)

        s_next = scores(0)
        for p in range(N_PAIRS):
            s_cur = s_next
            if p + 1 < N_PAIRS:
                s_next = scores(p + 1)
            finish(p, s_cur)


def _attn_prompt_call(q, k, v, rvec):
    s = q.shape[1]
    assert s % ATT_STEP == 0 and ATT_STEP >= BAND_PAST
    n = s // ATT_STEP
    cur = pl.BlockSpec((N_PAIRS, ATT_STEP, PAIR_W), lambda i: (0, i, 0))
    prev = pl.BlockSpec((N_PAIRS, ATT_STEP, PAIR_W), lambda i: (0, jnp.maximum(i - 1, 0), 0))
    return pl.pallas_call(
        _attn_prompt_kernel,
        grid=(n,),
        in_specs=[cur, prev, cur, prev, cur, _const_spec(rvec.shape)],
        out_specs=cur,
        out_shape=jax.ShapeDtypeStruct((N_PAIRS, s, PAIR_W), BF16),
        scratch_shapes=[pltpu.VMEM((N_HEADS * Q_BLOCK, K_BLOCK), F32)],
        compiler_params=_params(),
        name="attn_prompt",
    )(q, k, k, v, v, rvec)


def _attn_decode_kernel(q_ref, kn_ref, vn_ref, kc_ref, vc_ref, rvec_ref, o_ref, bias_ref):
    t = q_ref.shape[0]
    w = kc_ref.shape[1]
    kpad = bias_ref.shape[1]

    @pl.when(pl.program_id(0) == 0)
    def _():
        kj = lax.broadcasted_iota(jnp.int32, (t, REL_RING), 1)
        valid = kj < w + t
        for h in range(N_HEADS):
            bias_ref[h * t:(h + 1) * t, :] = _rel_bias_rows(rvec_ref, h, t, valid)[:, :kpad]

    q = q_ref[...]
    head_of_lane = lax.broadcasted_iota(jnp.int32, (t, D_MODEL), 1) >> 6
    qs = jnp.concatenate(
        [jnp.where(head_of_lane == h, q, jnp.zeros_like(q)) for h in range(N_HEADS)], axis=0)
    zeros = jnp.zeros((kpad - w - t, D_MODEL), BF16)
    kk = jnp.concatenate([kc_ref[0].astype(BF16), kn_ref[...], zeros], axis=0)
    vv = jnp.concatenate([vc_ref[0].astype(BF16), vn_ref[...], zeros], axis=0)
    eb, l = _softmax_weights(_dot_nt(qs, kk) + bias_ref[...])
    o = _weighted_values(eb, l, (vv,))
    out = jnp.zeros((t, D_MODEL), F32)
    for h in range(N_HEADS):
        out = out + jnp.where(head_of_lane == h, o[h * t:(h + 1) * t, :], 0.0)
    o_ref[...] = out.astype(BF16)


def _attn_decode_call(q, kn, vn, cache_k, cache_v, rvec, *, t):
    nb, w = cache_k.shape[0], cache_k.shape[1]
    assert w == BAND_PAST
    kpad = -(-(w + t) // LANES_V7X) * LANES_V7X
    assert kpad + t <= REL_RING
    new_spec = pl.BlockSpec((t, D_MODEL), lambda i: (i, 0))
    cache_spec = pl.BlockSpec((1, w, D_MODEL), lambda i: (i, 0, 0))
    return pl.pallas_call(
        _attn_decode_kernel,
        grid=(nb,),
        in_specs=[new_spec, new_spec, new_spec, cache_spec, cache_spec, _const_spec(rvec.shape)],
        out_specs=new_spec,
        out_shape=jax.ShapeDtypeStruct((nb * t, D_MODEL), BF16),
        scratch_shapes=[pltpu.VMEM((N_HEADS * t, kpad), F32)],
        compiler_params=_params(),
        name="attn_decode",
    )(q, kn, vn, cache_k, cache_v, rvec)


def _rel_vector(table):
    m = jnp.arange(REL_RING)
    d = jnp.where(m < K_BLOCK, m, m - REL_RING)
    idx = jnp.clip(BAND_PAST - d, -REL_MAX, REL_MAX) + REL_MAX
    return table[idx].T.astype(F32)


def _trunk(x, conv_prev, caches, wts, *, tm, n_sub, proj_tm, keep):
    s = x.shape[0]
    ln = lambda l: (wts["ln_mix_g"][l][None], wts["ln_mix_b"][l][None],
                    wts["ln_ffn_g"][l][None], wts["ln_ffn_b"][l][None])
    stream_len = 0 if conv_prev is None else s // conv_prev.shape[1]
    pairs = caches is None
    ztails = []
    mlp = lambda l: (_Weight(wts["w_up"], l), _Weight(wts["w_down"], l))
    for l in range(N_A):
        mix = (_Weight(wts["w_in_a"], l), _Weight(wts["conv_w_a"], l), _Weight(wts["w_out_a"], l))
        if stream_len:
            p2 = jnp.pad(conv_prev[l], ((0, 0), (0, stream_len - 2), (0, 0))).reshape(s, D_MODEL)
            p1 = jnp.pad(conv_prev[l][:, 1:], ((0, 0), (0, stream_len - 1), (0, 0))).reshape(s, D_MODEL)
            mix = (p1, p2) + mix
        x, zt = _layer_call(x, mix, ln(l), *mlp(l),
                            conv=True, stream_len=stream_len, pairs=False, tm=tm, n_sub=n_sub)
        ztails.append(zt)

    q_scale = HEAD_DIM ** -0.5 * LOG2E
    q, k, kf, v, vf = _proj_call(
        x, (_Weight(wts["w_q_b"], 0), _Weight(wts["w_k"]), _Weight(wts["w_v"])),
        (q_scale, 1.0, 1.0), (False, True, True), tm=proj_tm, keep=keep, pairs=pairs)
    n_b = DEPTH - N_A
    for j in range(n_b):
        l = N_A + j
        rvec = _rel_vector(wts["rel_bias_b"][j])
        if caches is None:
            a = _attn_prompt_call(q, k, v, rvec)
        else:
            a = _attn_decode_call(q, k, v, caches[0], caches[1], rvec, t=s // caches[0].shape[0])
        fuse_q = pairs and j + 1 < n_b
        outs = _layer_call(x, (a, _Weight(wts["w_o_b"], j)), ln(l), *mlp(l),
                           conv=False, stream_len=0, pairs=pairs, tm=tm, n_sub=n_sub,
                           w_q=_Weight(wts["w_q_b"], j + 1) if fuse_q else None,
                           q_scale=q_scale if fuse_q else 0.0)
        x = outs[0]
        if fuse_q:
            q = outs[1]
        elif j + 1 < n_b:
            (q,) = _proj_call(x, (_Weight(wts["w_q_b"], j + 1),), (q_scale,), (False,),
                              tm=proj_tm, keep=keep, pairs=pairs)
    return x, ztails, kf, vf


def kernel(x_prompt, x_sample, cache_conv, cache_k, cache_v, ln_mix_g, ln_mix_b, ln_ffn_g, ln_ffn_b,
           w_up, w_down, w_in_a, conv_w_a, w_out_a, w_k, w_v, w_q_b, w_o_b, rel_bias_b):
    assert x_prompt.shape[0] == 1
    wts = dict(
        ln_mix_g=ln_mix_g, ln_mix_b=ln_mix_b, ln_ffn_g=ln_ffn_g, ln_ffn_b=ln_ffn_b,
        w_up=w_up.astype(BF16), w_down=w_down.astype(BF16), w_in_a=w_in_a.astype(BF16),
        conv_w_a=conv_w_a, w_out_a=w_out_a.astype(BF16), w_k=w_k.astype(BF16), w_v=w_v.astype(BF16),
        w_q_b=w_q_b.astype(BF16), w_o_b=w_o_b.astype(BF16), rel_bias_b=rel_bias_b)

    seq = x_prompt.shape[1]
    keep = min(BAND_PAST, seq)
    yp, ztp, kfp, vfp = _trunk(x_prompt[0], None, None, wts, tm=512, n_sub=2, proj_tm=1024, keep=keep)
    conv_prompt = jnp.stack([zt[SUBLANES_V7X - 2:] for zt in ztp])[:, None]
    k_prompt = kfp.reshape(1, keep, N_HEADS, HEAD_DIM)
    v_prompt = vfp.reshape(1, keep, N_HEADS, HEAD_DIM)

    nb, t = x_sample.shape[0], x_sample.shape[1]
    w = cache_k.shape[1]
    ns = nb * t
    ys, zts, kfs, vfs = _trunk(
        x_sample.reshape(ns, D_MODEL), cache_conv,
        (cache_k.reshape(nb, w, D_MODEL), cache_v.reshape(nb, w, D_MODEL)),
        wts, tm=ns, n_sub=1, proj_tm=ns, keep=ns)
    conv_sample = jnp.stack([zt.reshape(nb, t, D_MODEL)[:, t - 2:] for zt in zts])
    k_sample = kfs.reshape(nb, t, N_HEADS, HEAD_DIM)
    v_sample = vfs.reshape(nb, t, N_HEADS, HEAD_DIM)
    return (yp[None], ys.reshape(nb, t, D_MODEL), conv_prompt, k_prompt, v_prompt,
            conv_sample, k_sample, v_sample)
```

```python
import dataclasses
import functools
import math

import jax
import jax.numpy as jnp
from jax import lax
from jax.experimental import pallas as pl
from jax.experimental.pallas import tpu as pltpu

D_MODEL = 1024
N_HEADS = 16
HEAD_DIM = 64
DEPTH = 4
N_A = 2
CHUNK = 64
N_PREV_CHUNKS = 8
BAND_PAST = N_PREV_CHUNKS * CHUNK
REL_MAX = 128
ALPHA = (2.0 * DEPTH) ** 0.25
LN_EPS = 1e-5
LOG2E = math.log2(math.e)

LANES_V7X = 128
SUBLANES_V7X = 8
VMEM_BYTES_V7X = 64 * 1024 * 1024
VMEM_LIMIT_BYTES = VMEM_BYTES_V7X - 8 * 1024 * 1024

PAIR_W = LANES_V7X
N_PAIRS = D_MODEL // PAIR_W
Q_BLOCK = 2 * CHUNK
K_BLOCK = BAND_PAST + Q_BLOCK
ATT_STEP = BAND_PAST
CONVERT_ROWS = 256
REL_RING = 1024
MASK_VALUE = -1e30

BF16 = jnp.bfloat16
F32 = jnp.float32


def _const_spec(shape):
    zeros = (0,) * len(shape)
    return pl.BlockSpec(shape, lambda i: zeros, pipeline_mode=pl.Buffered(1))


@dataclasses.dataclass(frozen=True)
class _Weight:
    array: jax.Array
    layer: int | None = None

    @property
    def shape(self):
        return self.array.shape if self.layer is None else self.array.shape[1:]

    def spec(self):
        if self.layer is None:
            return _const_spec(self.array.shape)
        index = (self.layer,) + (0,) * len(self.shape)
        return pl.BlockSpec((None,) + self.shape, lambda i: index, pipeline_mode=pl.Buffered(1))


def _params():
    return pltpu.CompilerParams(dimension_semantics=("arbitrary",), vmem_limit_bytes=VMEM_LIMIT_BYTES)


def _dot(a, b):
    return jnp.dot(a, b, preferred_element_type=F32)


def _dot_nt(a, b):
    return lax.dot_general(a, b, (((1,), (1,)), ((), ())), preferred_element_type=F32)


def _layer_norm(x, g, b):
    mu = jnp.mean(x, axis=-1, keepdims=True)
    xc = x - mu
    var = jnp.mean(xc * xc, axis=-1, keepdims=True)
    return xc * lax.rsqrt(var + LN_EPS) * g + b


def _store_pairs(ref, val):
    for p in range(N_PAIRS):
        ref[p] = val[:, p * PAIR_W:(p + 1) * PAIR_W].astype(BF16)


@dataclasses.dataclass(frozen=True)
class _LayerCfg:
    conv: bool
    stream_len: int
    pairs: bool
    tm: int
    n_sub: int
    q_scale: float
    convert: tuple[int, int] | None = None

    @property
    def streams(self):
        return self.stream_len > 0


def _convert_weight(src, dst, out, stage, in_sems, out_sem):
    n_rows, n_cols = dst.shape
    n = n_rows // CONVERT_ROWS

    def chunk_copy(c):
        return pltpu.make_async_copy(src.at[pl.ds(c * CONVERT_ROWS, CONVERT_ROWS), :],
                                     stage.at[c % 2, :, pl.ds(0, n_cols)], in_sems.at[c % 2])

    chunk_copy(0).start()
    for c in range(n):
        if c + 1 < n:
            chunk_copy(c + 1).start()
        chunk_copy(c).wait()
        dst[c * CONVERT_ROWS:(c + 1) * CONVERT_ROWS, :] = stage[c % 2, :, :n_cols].astype(BF16)
    writeback = pltpu.make_async_copy(dst, out, out_sem)
    writeback.start()
    return writeback


def _layer_kernel(*refs, cfg):
    it = iter(refs)
    x_ref = next(it)
    if cfg.conv:
        if cfg.streams:
            p1_ref, p2_ref = next(it), next(it)
        w_in_ref, cw_ref, w_mix_ref = next(it), next(it), next(it)
    else:
        a_ref, w_mix_ref = next(it), next(it)
    g1_ref, b1_ref, g2_ref, b2_ref = next(it), next(it), next(it), next(it)
    w_up_ref, w_down_ref = next(it), next(it)
    if cfg.q_scale:
        w_q_ref = next(it)
    y_ref = next(it)
    if cfg.q_scale:
        q_ref = next(it)
    if cfg.conv:
        ztail_ref = next(it)
    if cfg.convert:
        n_w = 4 if cfg.conv else 3
        outs = [next(it) for _ in range(n_w)]
    if cfg.conv and not cfg.streams:
        carry_ref = next(it)
    if cfg.convert:
        bufs = [next(it) for _ in range(n_w)]
        stage, in_sems, out_sems = next(it), next(it), next(it)
        mix_l, mlp_l = cfg.convert
        srcs = ([w_in_ref.at[mix_l]] if cfg.conv else []) + [
            w_mix_ref.at[mix_l], w_up_ref.at[mlp_l], w_down_ref.at[mlp_l]]
        writebacks = [_convert_weight(srcs[j], bufs[j], outs[j], stage, in_sems, out_sems.at[j])
                      for j in range(n_w)]
        if cfg.conv:
            w_in_ref = bufs[0]
        w_mix_ref, w_up_ref, w_down_ref = bufs[-3:]

    if cfg.conv and not cfg.streams:
        @pl.when(pl.program_id(0) == 0)
        def _():
            carry_ref[...] = jnp.zeros_like(carry_ref)
        prev = carry_ref[...]

    sm = cfg.tm // cfg.n_sub
    subs = range(cfg.n_sub)
    rows = [slice(sub * sm, (sub + 1) * sm) for sub in subs]
    x = [x_ref[r, :] for r in rows]
    if cfg.conv:
        bch = [_dot(x[i].astype(BF16), w_in_ref[...]) for i in subs]
        row = lax.broadcasted_iota(jnp.int32, (sm, D_MODEL), 0)
        cw = cw_ref[...]
        mix_in = []
        for i in subs:
            b_gate = bch[i][:, :D_MODEL]
            z = bch[i][:, D_MODEL:2 * D_MODEL] * bch[i][:, 2 * D_MODEL:]
            r1 = pltpu.roll(z, 1, 0)
            r2 = pltpu.roll(z, 2, 0)
            if cfg.streams:
                rm = row & (cfg.stream_len - 1)
                zm1 = jnp.where(rm == 0, p1_ref[rows[i], :], r1)
                zm2 = jnp.where(rm < 2, p2_ref[rows[i], :], r2)
                ztail_ref[rows[i], :] = z
            else:
                pm1 = prev[SUBLANES_V7X - 1:SUBLANES_V7X, :]
                pm2 = prev[SUBLANES_V7X - 2:SUBLANES_V7X - 1, :]
                zm1 = jnp.where(row == 0, pm1, r1)
                zm2 = jnp.where(row == 0, pm2, jnp.where(row == 1, pm1, r2))
                prev = z[sm - SUBLANES_V7X:, :]
            yc = cw[0:1, :] * zm2 + cw[1:2, :] * zm1 + cw[2:3, :] * z
            mix_in.append((b_gate * yc).astype(BF16))
        if not cfg.streams:
            carry_ref[...] = prev
            ztail_ref[...] = prev
    elif cfg.pairs:
        mix_in = [jnp.concatenate([a_ref[p, r, :] for p in range(N_PAIRS)], axis=-1) for r in rows]
    else:
        mix_in = [a_ref[r, :] for r in rows]
    h = [_dot(mix_in[i], w_mix_ref[...]) for i in subs]
    x1 = [_layer_norm(ALPHA * x[i] + h[i], g1_ref[...], b1_ref[...]) for i in subs]
    u = [_dot(x1[i].astype(BF16), w_up_ref[...]) for i in subs]
    u = [jnp.maximum(u[i], 0.0) for i in subs]
    m = [_dot((u[i] * u[i]).astype(BF16), w_down_ref[...]) for i in subs]
    y = [_layer_norm(ALPHA * x1[i] + m[i], g2_ref[...], b2_ref[...]) for i in subs]
    for i in subs:
        y_ref[rows[i], :] = y[i]
    if cfg.q_scale:
        q = [_dot(y[i].astype(BF16), w_q_ref[...]) * cfg.q_scale for i in subs]
        for i in subs:
            for p in range(N_PAIRS):
                q_ref[p, rows[i], :] = q[i][:, p * PAIR_W:(p + 1) * PAIR_W].astype(BF16)
    if cfg.convert:
        for writeback in writebacks:
            writeback.wait()


def _layer_call(x, mix_args, ln, w_up, w_down, *, conv, stream_len, pairs, tm, n_sub,
                w_q=None, q_scale=0.0, convert=False):
    s = x.shape[0]
    assert s % tm == 0 and tm % n_sub == 0
    assert stream_len & (stream_len - 1) == 0 and (stream_len == 0 or (tm // n_sub) % stream_len == 0)
    assert (w_q is None) == (q_scale == 0.0)
    n = s // tm
    streams = stream_len > 0
    row_spec = pl.BlockSpec((tm, D_MODEL), lambda i: (i, 0))
    pair_spec = pl.BlockSpec((N_PAIRS, tm, PAIR_W), lambda i: (0, i, 0))
    hbm_spec = pl.BlockSpec(memory_space=pl.ANY)
    w_spec = (lambda w: hbm_spec) if convert else (lambda w: w.spec())
    in_specs = [row_spec]
    args = [x]
    if conv:
        if streams:
            p1, p2, w_in, cw, w_mix = mix_args
            in_specs += [row_spec, row_spec]
            args += [p1, p2]
        else:
            w_in, cw, w_mix = mix_args
        in_specs += [w_spec(w_in), cw.spec(), w_spec(w_mix)]
        args += [w_in.array, cw.array, w_mix.array]
        converted = [w_in, w_mix, w_up, w_down]
    else:
        a, w_mix = mix_args
        in_specs += [pair_spec if pairs else row_spec, w_spec(w_mix)]
        args += [a, w_mix.array]
        converted = [w_mix, w_up, w_down]
    in_specs += [_const_spec((1, D_MODEL))] * 4 + [w_spec(w_up), w_spec(w_down)]
    args += list(ln) + [w_up.array, w_down.array]

    out_shape = [jax.ShapeDtypeStruct((s, D_MODEL), F32)]
    out_specs = [row_spec]
    if w_q is not None:
        in_specs.append(w_q.spec())
        args.append(w_q.array)
        out_shape.append(jax.ShapeDtypeStruct((N_PAIRS, s, PAIR_W), BF16))
        out_specs.append(pair_spec)
    scratch = []
    if conv:
        if streams:
            out_shape.append(jax.ShapeDtypeStruct((s, D_MODEL), F32))
            out_specs.append(row_spec)
        else:
            out_shape.append(jax.ShapeDtypeStruct((SUBLANES_V7X, D_MODEL), F32))
            out_specs.append(pl.BlockSpec((SUBLANES_V7X, D_MODEL), lambda i: (0, 0)))
            scratch.append(pltpu.VMEM((SUBLANES_V7X, D_MODEL), F32))
    if convert:
        assert n == 1 and all(w.layer is not None and w.shape[0] % CONVERT_ROWS == 0 for w in converted)
        out_shape += [jax.ShapeDtypeStruct(w.shape, BF16) for w in converted]
        out_specs += [hbm_spec] * len(converted)
        scratch += [pltpu.VMEM(w.shape, BF16) for w in converted]
        scratch += [pltpu.VMEM((2, CONVERT_ROWS, max(w.shape[1] for w in converted)), F32),
                    pltpu.SemaphoreType.DMA((2,)), pltpu.SemaphoreType.DMA((len(converted),))]
    cfg = _LayerCfg(conv=conv, stream_len=stream_len, pairs=pairs, tm=tm, n_sub=n_sub, q_scale=q_scale,
                    convert=(w_mix.layer, w_up.layer) if convert else None)
    return pl.pallas_call(
        functools.partial(_layer_kernel, cfg=cfg),
        grid=(n,),
        in_specs=in_specs,
        out_specs=out_specs,
        out_shape=out_shape,
        scratch_shapes=scratch,
        compiler_params=_params(),
        name="layer_conv" if conv else "layer_attn",
    )(*args)


def _proj_kernel(*refs, scales, tails, tail_rows, pairs):
    n = len(scales)
    x_ref = refs[0]
    w_refs = refs[1:1 + n]
    out_refs = list(refs[1 + n:])
    xb = x_ref[...].astype(BF16)
    tm = xb.shape[0]
    for j in range(n):
        pj = _dot(xb, w_refs[j][...])
        if scales[j] != 1.0:
            pj = pj * scales[j]
        if pairs:
            _store_pairs(out_refs.pop(0), pj)
        else:
            out_refs.pop(0)[...] = pj.astype(BF16)
        if tails[j]:
            out_refs.pop(0)[...] = pj[tm - tail_rows:, :]


def _proj_call(x, ws, scales, tails, *, tm, keep, pairs):
    s = x.shape[0]
    assert s % tm == 0
    n = s // tm
    tail_rows = min(tm, keep)
    assert keep % tail_rows == 0
    n_tail_blocks = keep // tail_rows
    row_spec = pl.BlockSpec((tm, D_MODEL), lambda i: (i, 0))
    pair_spec = pl.BlockSpec((N_PAIRS, tm, PAIR_W), lambda i: (0, i, 0))
    tail_spec = pl.BlockSpec((tail_rows, D_MODEL),
                             lambda i: (jnp.maximum(i - (n - n_tail_blocks), 0), 0))
    out_shape, out_specs = [], []
    for t in tails:
        if pairs:
            out_shape.append(jax.ShapeDtypeStruct((N_PAIRS, s, PAIR_W), BF16))
            out_specs.append(pair_spec)
        else:
            out_shape.append(jax.ShapeDtypeStruct((s, D_MODEL), BF16))
            out_specs.append(row_spec)
        if t:
            out_shape.append(jax.ShapeDtypeStruct((keep, D_MODEL), F32))
            out_specs.append(tail_spec)
    return pl.pallas_call(
        functools.partial(_proj_kernel, scales=tuple(scales), tails=tuple(tails),
                          tail_rows=tail_rows, pairs=pairs),
        grid=(n,),
        in_specs=[row_spec] + [w.spec() for w in ws],
        out_specs=out_specs,
        out_shape=out_shape,
        compiler_params=_params(),
        name="proj",
    )(x, *[w.array for w in ws])


def _rel_bias_rows(rvec_ref, h, rows, valid):
    row = jnp.broadcast_to(rvec_ref[h:h + 1, :], (rows, REL_RING))
    t = pltpu.roll(row, 0, 1, stride=1, stride_axis=0)
    return jnp.where(valid, t * LOG2E, MASK_VALUE)


def _softmax_weights(s):
    m = jnp.max(s, axis=-1, keepdims=True)
    e = jnp.exp2(s - m)
    return e.astype(BF16), jnp.sum(e, axis=-1, keepdims=True)


def _weighted_values(eb, l, v_parts):
    o = None
    off = 0
    for v in v_parts:
        part = _dot(eb[:, off:off + v.shape[0]], v)
        o = part if o is None else o + part
        off += v.shape[0]
    return o / l


def _attn_prompt_kernel(q_ref, kp_ref, kc_ref, vp_ref, vc_ref, rvec_ref, o_ref, bias_ref):
    step = pl.program_id(0)
    low = lax.broadcasted_iota(jnp.int32, (Q_BLOCK, PAIR_W), 1) < HEAD_DIM
    n_blocks = ATT_STEP // Q_BLOCK
    missing = [max(BAND_PAST - b * Q_BLOCK, 0) for b in range(n_blocks)]

    def build_bias(first_key):
        qi = lax.broadcasted_iota(jnp.int32, (Q_BLOCK, REL_RING), 0)
        kj = lax.broadcasted_iota(jnp.int32, (Q_BLOCK, REL_RING), 1)
        start = (qi >> 6) << 6
        valid = (kj >= start) & (kj < start + (BAND_PAST + CHUNK)) & (kj >= first_key)
        for h in range(N_HEADS):
            bias_ref[h * Q_BLOCK:(h + 1) * Q_BLOCK, :] = (
                _rel_bias_rows(rvec_ref, h, Q_BLOCK, valid)[:, :K_BLOCK])

    for blk in range(n_blocks):
        if blk == 0:
            rebuild = (step == 0) | (step == 1) if missing[-1] else (step == 0)
        else:
            rebuild = (step == 0) if missing[blk] != missing[blk - 1] else None
        if rebuild is not None:
            pl.when(rebuild)(functools.partial(
                build_bias, jnp.where(step == 0, missing[blk], 0)))

        q_rows = slice(blk * Q_BLOCK, (blk + 1) * Q_BLOCK)
        past_lo = blk * Q_BLOCK
        cur_rows = slice(max(past_lo - BAND_PAST, 0), past_lo + Q_BLOCK)

        def window(prev_ref, cur_ref, p):
            parts = [prev_ref[p, past_lo:, :]] if past_lo < BAND_PAST else []
            return parts + [cur_ref[p, cur_rows, :]]

        def scores(p):
            qb = q_ref[p, q_rows, :]
            qs = jnp.concatenate([jnp.where(low, qb, jnp.zeros_like(qb)),
                                  jnp.where(low, jnp.zeros_like(qb), qb)], axis=0)
            s = jnp.concatenate([_dot_nt(qs, k) for k in window(kp_ref, kc_ref, p)], axis=-1)
            return s + bias_ref[2 * p * Q_BLOCK:2 * (p + 1) * Q_BLOCK, :]

        def finish(p, s):
            o = _weighted_values(*_softmax_weights(s), window(vp_ref, vc_ref, p))
            o_ref[p, q_rows, :] = jnp.where(low, o[:Q_BLOCK], o[Q_BLOCK:]).astype(BF16)

        s_next = scores(0)
        for p in range(N_PAIRS):
            s_cur = s_next
            if p + 1 < N_PAIRS:
                s_next = scores(p + 1)
            finish(p, s_cur)


def _attn_prompt_call(q, k, v, rvec):
    s = q.shape[1]
    assert s % ATT_STEP == 0 and ATT_STEP % BAND_PAST == 0
    n = s // ATT_STEP
    ratio = ATT_STEP // BAND_PAST
    cur = pl.BlockSpec((N_PAIRS, ATT_STEP, PAIR_W), lambda i: (0, i, 0))
    prev = pl.BlockSpec((N_PAIRS, BAND_PAST, PAIR_W), lambda i: (0, jnp.maximum(ratio * i - 1, 0), 0))
    return pl.pallas_call(
        _attn_prompt_kernel,
        grid=(n,),
        in_specs=[cur, prev, cur, prev, cur, _const_spec(rvec.shape)],
        out_specs=cur,
        out_shape=jax.ShapeDtypeStruct((N_PAIRS, s, PAIR_W), BF16),
        scratch_shapes=[pltpu.VMEM((N_HEADS * Q_BLOCK, K_BLOCK), F32)],
        compiler_params=_params(),
        name="attn_prompt",
    )(q, k, k, v, v, rvec)


def _attn_decode_kernel(q_ref, kn_ref, vn_ref, kc_ref, vc_ref, rvec_ref, o_ref, bias_ref):
    t = q_ref.shape[0]
    w = kc_ref.shape[1]
    kpad = bias_ref.shape[1]

    @pl.when(pl.program_id(0) == 0)
    def _():
        kj = lax.broadcasted_iota(jnp.int32, (t, REL_RING), 1)
        valid = kj < w + t
        for h in range(N_HEADS):
            bias_ref[h * t:(h + 1) * t, :] = _rel_bias_rows(rvec_ref, h, t, valid)[:, :kpad]

    q = q_ref[...]
    head_of_lane = lax.broadcasted_iota(jnp.int32, (t, D_MODEL), 1) >> 6
    qs = jnp.concatenate(
        [jnp.where(head_of_lane == h, q, jnp.zeros_like(q)) for h in range(N_HEADS)], axis=0)
    zeros = jnp.zeros((kpad - w - t, D_MODEL), BF16)
    kk = jnp.concatenate([kc_ref[0].astype(BF16), kn_ref[...], zeros], axis=0)
    vv = jnp.concatenate([vc_ref[0].astype(BF16), vn_ref[...], zeros], axis=0)
    eb, l = _softmax_weights(_dot_nt(qs, kk) + bias_ref[...])
    o = _weighted_values(eb, l, (vv,))
    out = jnp.zeros((t, D_MODEL), F32)
    for h in range(N_HEADS):
        out = out + jnp.where(head_of_lane == h, o[h * t:(h + 1) * t, :], 0.0)
    o_ref[...] = out.astype(BF16)


def _attn_decode_call(q, kn, vn, cache_k, cache_v, rvec, *, t):
    nb, w = cache_k.shape[0], cache_k.shape[1]
    assert w == BAND_PAST
    kpad = -(-(w + t) // LANES_V7X) * LANES_V7X
    assert kpad + t <= REL_RING
    new_spec = pl.BlockSpec((t, D_MODEL), lambda i: (i, 0))
    cache_spec = pl.BlockSpec((1, w, D_MODEL), lambda i: (i, 0, 0))
    return pl.pallas_call(
        _attn_decode_kernel,
        grid=(nb,),
        in_specs=[new_spec, new_spec, new_spec, cache_spec, cache_spec, _const_spec(rvec.shape)],
        out_specs=new_spec,
        out_shape=jax.ShapeDtypeStruct((nb * t, D_MODEL), BF16),
        scratch_shapes=[pltpu.VMEM((N_HEADS * t, kpad), F32)],
        compiler_params=_params(),
        name="attn_decode",
    )(q, kn, vn, cache_k, cache_v, rvec)


def _rel_vector(table):
    m = jnp.arange(REL_RING)
    d = jnp.where(m < K_BLOCK, m, m - REL_RING)
    idx = jnp.clip(BAND_PAST - d, -REL_MAX, REL_MAX) + REL_MAX
    return table[idx].T.astype(F32)


def _trunk(x, conv_prev, caches, wts, layer_bf16, *, tm, n_sub, proj_tm, keep):
    s = x.shape[0]
    ln = lambda l: (wts["ln_mix_g"][l][None], wts["ln_mix_b"][l][None],
                    wts["ln_ffn_g"][l][None], wts["ln_ffn_b"][l][None])
    stream_len = 0 if conv_prev is None else s // conv_prev.shape[1]
    pairs = caches is None
    convert = layer_bf16 is None
    converted = []
    ztails = []
    for l in range(N_A):
        if convert:
            w_in, w_out = _Weight(wts["w_in_a"], l), _Weight(wts["w_out_a"], l)
            w_up, w_down = _Weight(wts["w_up"], l), _Weight(wts["w_down"], l)
        else:
            w_in, w_out, w_up, w_down = (_Weight(w) for w in layer_bf16[l])
        mix = (w_in, _Weight(wts["conv_w_a"], l), w_out)
        if stream_len:
            p2 = jnp.pad(conv_prev[l], ((0, 0), (0, stream_len - 2), (0, 0))).reshape(s, D_MODEL)
            p1 = jnp.pad(conv_prev[l][:, 1:], ((0, 0), (0, stream_len - 1), (0, 0))).reshape(s, D_MODEL)
            mix = (p1, p2) + mix
        x, zt, *bf16 = _layer_call(x, mix, ln(l), w_up, w_down, conv=True, stream_len=stream_len,
                                   pairs=False, tm=tm, n_sub=n_sub, convert=convert)
        ztails.append(zt)
        converted.append(bf16)

    q_scale = HEAD_DIM ** -0.5 * LOG2E
    q, k, kf, v, vf = _proj_call(
        x, (_Weight(wts["w_q_bf16"], 0), _Weight(wts["w_k_bf16"]), _Weight(wts["w_v_bf16"])),
        (q_scale, 1.0, 1.0), (False, True, True), tm=proj_tm, keep=keep, pairs=pairs)
    n_b = DEPTH - N_A
    for j in range(n_b):
        l = N_A + j
        rvec = _rel_vector(wts["rel_bias_b"][j])
        if caches is None:
            a = _attn_prompt_call(q, k, v, rvec)
        else:
            a = _attn_decode_call(q, k, v, caches[0], caches[1], rvec, t=s // caches[0].shape[0])
        if convert:
            w_o = _Weight(wts["w_o_b"], j)
            w_up, w_down = _Weight(wts["w_up"], l), _Weight(wts["w_down"], l)
        else:
            w_o, w_up, w_down = (_Weight(w) for w in layer_bf16[l])
        fuse_q = pairs and j + 1 < n_b
        outs = _layer_call(x, (a, w_o), ln(l), w_up, w_down,
                           conv=False, stream_len=0, pairs=pairs, tm=tm, n_sub=n_sub,
                           w_q=_Weight(wts["w_q_bf16"], j + 1) if fuse_q else None,
                           q_scale=q_scale if fuse_q else 0.0, convert=convert)
        x = outs[0]
        if fuse_q:
            q = outs[1]
        elif j + 1 < n_b:
            (q,) = _proj_call(x, (_Weight(wts["w_q_bf16"], j + 1),), (q_scale,), (False,),
                              tm=proj_tm, keep=keep, pairs=pairs)
        converted.append(outs[len(outs) - 3:] if convert else [])
    return x, ztails, kf, vf, converted


def kernel(x_prompt, x_sample, cache_conv, cache_k, cache_v, ln_mix_g, ln_mix_b, ln_ffn_g, ln_ffn_b,
           w_up, w_down, w_in_a, conv_w_a, w_out_a, w_k, w_v, w_q_b, w_o_b, rel_bias_b):
    assert x_prompt.shape[0] == 1
    wts = dict(
        ln_mix_g=ln_mix_g, ln_mix_b=ln_mix_b, ln_ffn_g=ln_ffn_g, ln_ffn_b=ln_ffn_b,
        w_up=w_up, w_down=w_down, w_in_a=w_in_a, conv_w_a=conv_w_a, w_out_a=w_out_a, w_o_b=w_o_b,
        w_k_bf16=w_k.astype(BF16), w_v_bf16=w_v.astype(BF16), w_q_bf16=w_q_b.astype(BF16),
        rel_bias_b=rel_bias_b)

    nb, t = x_sample.shape[0], x_sample.shape[1]
    w = cache_k.shape[1]
    ns = nb * t
    ys, zts, kfs, vfs, layer_bf16 = _trunk(
        x_sample.reshape(ns, D_MODEL), cache_conv,
        (cache_k.reshape(nb, w, D_MODEL), cache_v.reshape(nb, w, D_MODEL)),
        wts, None, tm=ns, n_sub=1, proj_tm=ns, keep=ns)
    conv_sample = jnp.stack([zt.reshape(nb, t, D_MODEL)[:, t - 2:] for zt in zts])
    k_sample = kfs.reshape(nb, t, N_HEADS, HEAD_DIM)
    v_sample = vfs.reshape(nb, t, N_HEADS, HEAD_DIM)

    seq = x_prompt.shape[1]
    keep = min(BAND_PAST, seq)
    yp, ztp, kfp, vfp, _ = _trunk(x_prompt[0], None, None, wts, layer_bf16,
                                  tm=512, n_sub=2, proj_tm=1024, keep=keep)
    conv_prompt = jnp.stack([zt[SUBLANES_V7X - 2:] for zt in ztp])[:, None]
    k_prompt = kfp.reshape(1, keep, N_HEADS, HEAD_DIM)
    v_prompt = vfp.reshape(1, keep, N_HEADS, HEAD_DIM)
    return (yp[None], ys.reshape(nb, t, D_MODEL), conv_prompt, k_prompt, v_prompt,
            conv_sample, k_sample, v_sample)
```

```python
import dataclasses
import functools
import math

import jax
import jax.numpy as jnp
from jax import lax
from jax.experimental import pallas as pl
from jax.experimental.pallas import tpu as pltpu

D_MODEL = 1024
N_HEADS = 16
HEAD_DIM = 64
DEPTH = 4
N_A = 2
CHUNK = 64
N_PREV_CHUNKS = 8
BAND_PAST = N_PREV_CHUNKS * CHUNK
REL_MAX = 128
ALPHA = (2.0 * DEPTH) ** 0.25
LN_EPS = 1e-5
LOG2E = math.log2(math.e)

LANES_V7X = 128
SUBLANES_V7X = 8
VMEM_BYTES_V7X = 64 * 1024 * 1024
VMEM_LIMIT_BYTES = VMEM_BYTES_V7X - 8 * 1024 * 1024

PAIR_W = LANES_V7X
N_PAIRS = D_MODEL // PAIR_W
Q_BLOCK = 2 * CHUNK
K_BLOCK = BAND_PAST + Q_BLOCK
ATT_STEP = BAND_PAST
CONVERT_CHUNK_BYTES = 2 * 1024 * 1024
CONVERT_SLOTS = 3
REL_RING = 1024
MASK_VALUE = -1e30

BF16 = jnp.bfloat16
F32 = jnp.float32


def _const_spec(shape):
    zeros = (0,) * len(shape)
    return pl.BlockSpec(shape, lambda i: zeros, pipeline_mode=pl.Buffered(1))


@dataclasses.dataclass(frozen=True)
class _Weight:
    array: jax.Array
    layer: int | None = None

    @property
    def shape(self):
        return self.array.shape if self.layer is None else self.array.shape[1:]

    def spec(self):
        if self.layer is None:
            return _const_spec(self.array.shape)
        index = (self.layer,) + (0,) * len(self.shape)
        return pl.BlockSpec((None,) + self.shape, lambda i: index, pipeline_mode=pl.Buffered(1))


def _params():
    return pltpu.CompilerParams(dimension_semantics=("arbitrary",), vmem_limit_bytes=VMEM_LIMIT_BYTES)


def _dot(a, b):
    return jnp.dot(a, b, preferred_element_type=F32)


def _dot_nt(a, b):
    return lax.dot_general(a, b, (((1,), (1,)), ((), ())), preferred_element_type=F32)


def _layer_norm(x, g, b):
    mu = jnp.mean(x, axis=-1, keepdims=True)
    xc = x - mu
    var = jnp.mean(xc * xc, axis=-1, keepdims=True)
    return xc * lax.rsqrt(var + LN_EPS) * g + b


def _store_pairs(ref, val):
    for p in range(N_PAIRS):
        ref[p] = val[:, p * PAIR_W:(p + 1) * PAIR_W].astype(BF16)


@dataclasses.dataclass(frozen=True)
class _LayerCfg:
    conv: bool
    stream_len: int
    pairs: bool
    tm: int
    n_sub: int
    q_scale: float
    convert: tuple[int, int] | None = None

    @property
    def streams(self):
        return self.stream_len > 0


def _convert_weights(srcs, dsts, outs, stages, in_sems, out_sems):
    chunks = []
    per_pool = [0, 0]
    for j, dst in enumerate(dsts):
        pool = 0 if dst.shape[1] > D_MODEL else 1
        rows = stages[pool].shape[1]
        for r0 in range(0, dst.shape[0], rows):
            chunks.append((pool, per_pool[pool], j, r0))
            per_pool[pool] += 1

    def chunk_copy(pool, k, j, r0):
        rows, cols = stages[pool].shape[1], dsts[j].shape[1]
        slot = k % CONVERT_SLOTS
        return pltpu.make_async_copy(srcs[j].at[pl.ds(r0, rows), :],
                                     stages[pool].at[slot, :, pl.ds(0, cols)], in_sems[pool].at[slot])

    started = 0
    done = [0, 0]
    writebacks = []
    for i, (pool, k, j, r0) in enumerate(chunks):
        while started < len(chunks) and chunks[started][1] < done[chunks[started][0]] + CONVERT_SLOTS:
            chunk_copy(*chunks[started]).start()
            started += 1
        chunk_copy(pool, k, j, r0).wait()
        rows, cols = stages[pool].shape[1], dsts[j].shape[1]
        dsts[j][r0:r0 + rows, :] = stages[pool][k % CONVERT_SLOTS, :, :cols].astype(BF16)
        done[pool] += 1
        if i + 1 == len(chunks) or chunks[i + 1][2] != j:
            writebacks.append(pltpu.make_async_copy(dsts[j], outs[j], out_sems.at[j]))
            writebacks[-1].start()
    return writebacks


def _layer_kernel(*refs, cfg):
    it = iter(refs)
    x_ref = next(it)
    if cfg.conv:
        if cfg.streams:
            p1_ref, p2_ref = next(it), next(it)
        w_in_ref, cw_ref, w_mix_ref = next(it), next(it), next(it)
    else:
        a_ref, w_mix_ref = next(it), next(it)
    g1_ref, b1_ref, g2_ref, b2_ref = next(it), next(it), next(it), next(it)
    w_up_ref, w_down_ref = next(it), next(it)
    if cfg.q_scale:
        w_q_ref = next(it)
    y_ref = next(it)
    if cfg.q_scale:
        q_ref = next(it)
    if cfg.conv:
        ztail_ref = next(it)
    if cfg.convert:
        n_w = 4 if cfg.conv else 3
        outs = [next(it) for _ in range(n_w)]
    if cfg.conv and not cfg.streams:
        carry_ref = next(it)
    if cfg.convert:
        bufs = [next(it) for _ in range(n_w)]
        stages, in_sems, out_sems = (next(it), next(it)), (next(it), next(it)), next(it)
        mix_l, mlp_l = cfg.convert
        srcs = ([w_in_ref.at[mix_l]] if cfg.conv else []) + [
            w_mix_ref.at[mix_l], w_up_ref.at[mlp_l], w_down_ref.at[mlp_l]]
        writebacks = _convert_weights(srcs, bufs, outs, stages, in_sems, out_sems)
        if cfg.conv:
            w_in_ref = bufs[0]
        w_mix_ref, w_up_ref, w_down_ref = bufs[-3:]

    if cfg.conv and not cfg.streams:
        @pl.when(pl.program_id(0) == 0)
        def _():
            carry_ref[...] = jnp.zeros_like(carry_ref)
        prev = carry_ref[...]

    sm = cfg.tm // cfg.n_sub
    subs = range(cfg.n_sub)
    rows = [slice(sub * sm, (sub + 1) * sm) for sub in subs]
    x = [x_ref[r, :] for r in rows]
    if cfg.conv:
        bch = [_dot(x[i].astype(BF16), w_in_ref[...]) for i in subs]
        row = lax.broadcasted_iota(jnp.int32, (sm, D_MODEL), 0)
        cw = cw_ref[...]
        mix_in = []
        for i in subs:
            b_gate = bch[i][:, :D_MODEL]
            z = bch[i][:, D_MODEL:2 * D_MODEL] * bch[i][:, 2 * D_MODEL:]
            r1 = pltpu.roll(z, 1, 0)
            r2 = pltpu.roll(z, 2, 0)
            if cfg.streams:
                rm = row & (cfg.stream_len - 1)
                zm1 = jnp.where(rm == 0, p1_ref[rows[i], :], r1)
                zm2 = jnp.where(rm < 2, p2_ref[rows[i], :], r2)
                ztail_ref[rows[i], :] = z
            else:
                pm1 = prev[SUBLANES_V7X - 1:SUBLANES_V7X, :]
                pm2 = prev[SUBLANES_V7X - 2:SUBLANES_V7X - 1, :]
                zm1 = jnp.where(row == 0, pm1, r1)
                zm2 = jnp.where(row == 0, pm2, jnp.where(row == 1, pm1, r2))
                prev = z[sm - SUBLANES_V7X:, :]
            yc = cw[0:1, :] * zm2 + cw[1:2, :] * zm1 + cw[2:3, :] * z
            mix_in.append((b_gate * yc).astype(BF16))
        if not cfg.streams:
            carry_ref[...] = prev
            ztail_ref[...] = prev
    elif cfg.pairs:
        mix_in = [jnp.concatenate([a_ref[p, r, :] for p in range(N_PAIRS)], axis=-1) for r in rows]
    else:
        mix_in = [a_ref[r, :] for r in rows]
    h = [_dot(mix_in[i], w_mix_ref[...]) for i in subs]
    x1 = [_layer_norm(ALPHA * x[i] + h[i], g1_ref[...], b1_ref[...]) for i in subs]
    u = [_dot(x1[i].astype(BF16), w_up_ref[...]) for i in subs]
    u = [jnp.maximum(u[i], 0.0) for i in subs]
    m = [_dot((u[i] * u[i]).astype(BF16), w_down_ref[...]) for i in subs]
    y = [_layer_norm(ALPHA * x1[i] + m[i], g2_ref[...], b2_ref[...]) for i in subs]
    for i in subs:
        y_ref[rows[i], :] = y[i]
    if cfg.q_scale:
        q = [_dot(y[i].astype(BF16), w_q_ref[...]) * cfg.q_scale for i in subs]
        for i in subs:
            for p in range(N_PAIRS):
                q_ref[p, rows[i], :] = q[i][:, p * PAIR_W:(p + 1) * PAIR_W].astype(BF16)
    if cfg.convert:
        for writeback in writebacks:
            writeback.wait()


def _layer_call(x, mix_args, ln, w_up, w_down, *, conv, stream_len, pairs, tm, n_sub,
                w_q=None, q_scale=0.0, convert=False):
    s = x.shape[0]
    assert s % tm == 0 and tm % n_sub == 0
    assert stream_len & (stream_len - 1) == 0 and (stream_len == 0 or (tm // n_sub) % stream_len == 0)
    assert (w_q is None) == (q_scale == 0.0)
    n = s // tm
    streams = stream_len > 0
    row_spec = pl.BlockSpec((tm, D_MODEL), lambda i: (i, 0))
    pair_spec = pl.BlockSpec((N_PAIRS, tm, PAIR_W), lambda i: (0, i, 0))
    hbm_spec = pl.BlockSpec(memory_space=pl.ANY)
    w_spec = (lambda w: hbm_spec) if convert else (lambda w: w.spec())
    in_specs = [row_spec]
    args = [x]
    if conv:
        if streams:
            p1, p2, w_in, cw, w_mix = mix_args
            in_specs += [row_spec, row_spec]
            args += [p1, p2]
        else:
            w_in, cw, w_mix = mix_args
        in_specs += [w_spec(w_in), cw.spec(), w_spec(w_mix)]
        args += [w_in.array, cw.array, w_mix.array]
        converted = [w_in, w_mix, w_up, w_down]
    else:
        a, w_mix = mix_args
        in_specs += [pair_spec if pairs else row_spec, w_spec(w_mix)]
        args += [a, w_mix.array]
        converted = [w_mix, w_up, w_down]
    in_specs += [_const_spec((1, D_MODEL))] * 4 + [w_spec(w_up), w_spec(w_down)]
    args += list(ln) + [w_up.array, w_down.array]

    out_shape = [jax.ShapeDtypeStruct((s, D_MODEL), F32)]
    out_specs = [row_spec]
    if w_q is not None:
        in_specs.append(w_q.spec())
        args.append(w_q.array)
        out_shape.append(jax.ShapeDtypeStruct((N_PAIRS, s, PAIR_W), BF16))
        out_specs.append(pair_spec)
    scratch = []
    if conv:
        if streams:
            out_shape.append(jax.ShapeDtypeStruct((s, D_MODEL), F32))
            out_specs.append(row_spec)
        else:
            out_shape.append(jax.ShapeDtypeStruct((SUBLANES_V7X, D_MODEL), F32))
            out_specs.append(pl.BlockSpec((SUBLANES_V7X, D_MODEL), lambda i: (0, 0)))
            scratch.append(pltpu.VMEM((SUBLANES_V7X, D_MODEL), F32))
    if convert:
        assert n == 1 and all(w.layer is not None for w in converted)
        out_shape += [jax.ShapeDtypeStruct(w.shape, BF16) for w in converted]
        out_specs += [hbm_spec] * len(converted)
        scratch += [pltpu.VMEM(w.shape, BF16) for w in converted]
        wide = max(w.shape[1] for w in converted)
        chunk_rows = (CONVERT_CHUNK_BYTES // (4 * wide), CONVERT_CHUNK_BYTES // (4 * D_MODEL))
        assert all(w.shape[0] % chunk_rows[0 if w.shape[1] > D_MODEL else 1] == 0 for w in converted)
        scratch += [pltpu.VMEM((CONVERT_SLOTS, chunk_rows[0], wide), F32),
                    pltpu.VMEM((CONVERT_SLOTS, chunk_rows[1], D_MODEL), F32),
                    pltpu.SemaphoreType.DMA((CONVERT_SLOTS,)), pltpu.SemaphoreType.DMA((CONVERT_SLOTS,)),
                    pltpu.SemaphoreType.DMA((len(converted),))]
    cfg = _LayerCfg(conv=conv, stream_len=stream_len, pairs=pairs, tm=tm, n_sub=n_sub, q_scale=q_scale,
                    convert=(w_mix.layer, w_up.layer) if convert else None)
    return pl.pallas_call(
        functools.partial(_layer_kernel, cfg=cfg),
        grid=(n,),
        in_specs=in_specs,
        out_specs=out_specs,
        out_shape=out_shape,
        scratch_shapes=scratch,
        compiler_params=_params(),
        name="layer_conv" if conv else "layer_attn",
    )(*args)


def _proj_kernel(*refs, scales, tails, tail_rows, pairs):
    n = len(scales)
    x_ref = refs[0]
    w_refs = refs[1:1 + n]
    out_refs = list(refs[1 + n:])
    xb = x_ref[...].astype(BF16)
    tm = xb.shape[0]
    for j in range(n):
        pj = _dot(xb, w_refs[j][...])
        if scales[j] != 1.0:
            pj = pj * scales[j]
        if pairs:
            _store_pairs(out_refs.pop(0), pj)
        else:
            out_refs.pop(0)[...] = pj.astype(BF16)
        if tails[j]:
            out_refs.pop(0)[...] = pj[tm - tail_rows:, :]


def _proj_call(x, ws, scales, tails, *, tm, keep, pairs):
    s = x.shape[0]
    assert s % tm == 0
    n = s // tm
    tail_rows = min(tm, keep)
    assert keep % tail_rows == 0
    n_tail_blocks = keep // tail_rows
    row_spec = pl.BlockSpec((tm, D_MODEL), lambda i: (i, 0))
    pair_spec = pl.BlockSpec((N_PAIRS, tm, PAIR_W), lambda i: (0, i, 0))
    tail_spec = pl.BlockSpec((tail_rows, D_MODEL),
                             lambda i: (jnp.maximum(i - (n - n_tail_blocks), 0), 0))
    out_shape, out_specs = [], []
    for t in tails:
        if pairs:
            out_shape.append(jax.ShapeDtypeStruct((N_PAIRS, s, PAIR_W), BF16))
            out_specs.append(pair_spec)
        else:
            out_shape.append(jax.ShapeDtypeStruct((s, D_MODEL), BF16))
            out_specs.append(row_spec)
        if t:
            out_shape.append(jax.ShapeDtypeStruct((keep, D_MODEL), F32))
            out_specs.append(tail_spec)
    return pl.pallas_call(
        functools.partial(_proj_kernel, scales=tuple(scales), tails=tuple(tails),
                          tail_rows=tail_rows, pairs=pairs),
        grid=(n,),
        in_specs=[row_spec] + [w.spec() for w in ws],
        out_specs=out_specs,
        out_shape=out_shape,
        compiler_params=_params(),
        name="proj",
    )(x, *[w.array for w in ws])


def _rel_bias_rows(rvec_ref, h, rows, valid):
    row = jnp.broadcast_to(rvec_ref[h:h + 1, :], (rows, REL_RING))
    t = pltpu.roll(row, 0, 1, stride=1, stride_axis=0)
    return jnp.where(valid, t * LOG2E, MASK_VALUE)


def _softmax_weights(s):
    m = jnp.max(s, axis=-1, keepdims=True)
    e = jnp.exp2(s - m)
    return e.astype(BF16), jnp.sum(e, axis=-1, keepdims=True)


def _weighted_values(eb, l, v_parts):
    o = None
    off = 0
    for v in v_parts:
        part = _dot(eb[:, off:off + v.shape[0]], v)
        o = part if o is None else o + part
        off += v.shape[0]
    return o / l


def _attn_prompt_kernel(q_ref, kp_ref, kc_ref, vp_ref, vc_ref, rvec_ref, o_ref, bias_ref):
    step = pl.program_id(0)
    low = lax.broadcasted_iota(jnp.int32, (Q_BLOCK, PAIR_W), 1) < HEAD_DIM
    n_blocks = ATT_STEP // Q_BLOCK
    missing = [max(BAND_PAST - b * Q_BLOCK, 0) for b in range(n_blocks)]

    def build_bias(first_key):
        qi = lax.broadcasted_iota(jnp.int32, (Q_BLOCK, REL_RING), 0)
        kj = lax.broadcasted_iota(jnp.int32, (Q_BLOCK, REL_RING), 1)
        start = (qi >> 6) << 6
        valid = (kj >= start) & (kj < start + (BAND_PAST + CHUNK)) & (kj >= first_key)
        for h in range(N_HEADS):
            bias_ref[h * Q_BLOCK:(h + 1) * Q_BLOCK, :] = (
                _rel_bias_rows(rvec_ref, h, Q_BLOCK, valid)[:, :K_BLOCK])

    for blk in range(n_blocks):
        if blk == 0:
            rebuild = (step == 0) | (step == 1) if missing[-1] else (step == 0)
        else:
            rebuild = (step == 0) if missing[blk] != missing[blk - 1] else None
        if rebuild is not None:
            pl.when(rebuild)(functools.partial(
                build_bias, jnp.where(step == 0, missing[blk], 0)))

        q_rows = slice(blk * Q_BLOCK, (blk + 1) * Q_BLOCK)
        past_lo = blk * Q_BLOCK
        cur_rows = slice(max(past_lo - BAND_PAST, 0), past_lo + Q_BLOCK)

        def window(prev_ref, cur_ref, p):
            parts = [prev_ref[p, past_lo:, :]] if past_lo < BAND_PAST else []
            return parts + [cur_ref[p, cur_rows, :]]

        def scores(p):
            qb = q_ref[p, q_rows, :]
            qs = jnp.concatenate([jnp.where(low, qb, jnp.zeros_like(qb)),
                                  jnp.where(low, jnp.zeros_like(qb), qb)], axis=0)
            s = jnp.concatenate([_dot_nt(qs, k) for k in window(kp_ref, kc_ref, p)], axis=-1)
            return s + bias_ref[2 * p * Q_BLOCK:2 * (p + 1) * Q_BLOCK, :]

        def finish(p, s):
            o = _weighted_values(*_softmax_weights(s), window(vp_ref, vc_ref, p))
            o_ref[p, q_rows, :] = jnp.where(low, o[:Q_BLOCK], o[Q_BLOCK:]).astype(BF16)

        s_next = scores(0)
        for p in range(N_PAIRS):
            s_cur = s_next
            if p + 1 < N_PAIRS:
                s_next = scores(p + 1)
            finish(p, s_cur)


def _attn_prompt_call(q, k, v, rvec):
    s = q.shape[1]
    assert s % ATT_STEP == 0 and ATT_STEP % BAND_PAST == 0
    n = s // ATT_STEP
    ratio = ATT_STEP // BAND_PAST
    cur = pl.BlockSpec((N_PAIRS, ATT_STEP, PAIR_W), lambda i: (0, i, 0))
    prev = pl.BlockSpec((N_PAIRS, BAND_PAST, PAIR_W), lambda i: (0, jnp.maximum(ratio * i - 1, 0), 0))
    return pl.pallas_call(
        _attn_prompt_kernel,
        grid=(n,),
        in_specs=[cur, prev, cur, prev, cur, _const_spec(rvec.shape)],
        out_specs=cur,
        out_shape=jax.ShapeDtypeStruct((N_PAIRS, s, PAIR_W), BF16),
        scratch_shapes=[pltpu.VMEM((N_HEADS * Q_BLOCK, K_BLOCK), F32)],
        compiler_params=_params(),
        name="attn_prompt",
    )(q, k, k, v, v, rvec)


def _attn_decode_kernel(q_ref, kn_ref, vn_ref, kc_ref, vc_ref, rvec_ref, o_ref, bias_ref):
    t = q_ref.shape[0]
    w = kc_ref.shape[1]
    kpad = bias_ref.shape[1]

    @pl.when(pl.program_id(0) == 0)
    def _():
        kj = lax.broadcasted_iota(jnp.int32, (t, REL_RING), 1)
        valid = kj < w + t
        for h in range(N_HEADS):
            bias_ref[h * t:(h + 1) * t, :] = _rel_bias_rows(rvec_ref, h, t, valid)[:, :kpad]

    q = q_ref[...]
    head_of_lane = lax.broadcasted_iota(jnp.int32, (t, D_MODEL), 1) >> 6
    qs = jnp.concatenate(
        [jnp.where(head_of_lane == h, q, jnp.zeros_like(q)) for h in range(N_HEADS)], axis=0)
    zeros = jnp.zeros((kpad - w - t, D_MODEL), BF16)
    kk = jnp.concatenate([kc_ref[0].astype(BF16), kn_ref[...], zeros], axis=0)
    vv = jnp.concatenate([vc_ref[0].astype(BF16), vn_ref[...], zeros], axis=0)
    eb, l = _softmax_weights(_dot_nt(qs, kk) + bias_ref[...])
    o = _weighted_values(eb, l, (vv,))
    out = jnp.zeros((t, D_MODEL), F32)
    for h in range(N_HEADS):
        out = out + jnp.where(head_of_lane == h, o[h * t:(h + 1) * t, :], 0.0)
    o_ref[...] = out.astype(BF16)


def _attn_decode_call(q, kn, vn, cache_k, cache_v, rvec, *, t):
    nb, w = cache_k.shape[0], cache_k.shape[1]
    assert w == BAND_PAST
    kpad = -(-(w + t) // LANES_V7X) * LANES_V7X
    assert kpad + t <= REL_RING
    new_spec = pl.BlockSpec((t, D_MODEL), lambda i: (i, 0))
    cache_spec = pl.BlockSpec((1, w, D_MODEL), lambda i: (i, 0, 0))
    return pl.pallas_call(
        _attn_decode_kernel,
        grid=(nb,),
        in_specs=[new_spec, new_spec, new_spec, cache_spec, cache_spec, _const_spec(rvec.shape)],
        out_specs=new_spec,
        out_shape=jax.ShapeDtypeStruct((nb * t, D_MODEL), BF16),
        scratch_shapes=[pltpu.VMEM((N_HEADS * t, kpad), F32)],
        compiler_params=_params(),
        name="attn_decode",
    )(q, kn, vn, cache_k, cache_v, rvec)


def _rel_vector(table):
    m = jnp.arange(REL_RING)
    d = jnp.where(m < K_BLOCK, m, m - REL_RING)
    idx = jnp.clip(BAND_PAST - d, -REL_MAX, REL_MAX) + REL_MAX
    return table[idx].T.astype(F32)


def _trunk(x, conv_prev, caches, wts, layer_bf16, *, tm, n_sub, proj_tm, keep):
    s = x.shape[0]
    ln = lambda l: (wts["ln_mix_g"][l][None], wts["ln_mix_b"][l][None],
                    wts["ln_ffn_g"][l][None], wts["ln_ffn_b"][l][None])
    stream_len = 0 if conv_prev is None else s // conv_prev.shape[1]
    pairs = caches is None
    convert = layer_bf16 is None
    converted = []
    ztails = []
    for l in range(N_A):
        if convert:
            w_in, w_out = _Weight(wts["w_in_a"], l), _Weight(wts["w_out_a"], l)
            w_up, w_down = _Weight(wts["w_up"], l), _Weight(wts["w_down"], l)
        else:
            w_in, w_out, w_up, w_down = (_Weight(w) for w in layer_bf16[l])
        mix = (w_in, _Weight(wts["conv_w_a"], l), w_out)
        if stream_len:
            p2 = jnp.pad(conv_prev[l], ((0, 0), (0, stream_len - 2), (0, 0))).reshape(s, D_MODEL)
            p1 = jnp.pad(conv_prev[l][:, 1:], ((0, 0), (0, stream_len - 1), (0, 0))).reshape(s, D_MODEL)
            mix = (p1, p2) + mix
        x, zt, *bf16 = _layer_call(x, mix, ln(l), w_up, w_down, conv=True, stream_len=stream_len,
                                   pairs=False, tm=tm, n_sub=n_sub, convert=convert)
        ztails.append(zt)
        converted.append(bf16)

    q_scale = HEAD_DIM ** -0.5 * LOG2E
    q, k, kf, v, vf = _proj_call(
        x, (_Weight(wts["w_q_bf16"], 0), _Weight(wts["w_k_bf16"]), _Weight(wts["w_v_bf16"])),
        (q_scale, 1.0, 1.0), (False, True, True), tm=proj_tm, keep=keep, pairs=pairs)
    n_b = DEPTH - N_A
    for j in range(n_b):
        l = N_A + j
        rvec = _rel_vector(wts["rel_bias_b"][j])
        if caches is None:
            a = _attn_prompt_call(q, k, v, rvec)
        else:
            a = _attn_decode_call(q, k, v, caches[0], caches[1], rvec, t=s // caches[0].shape[0])
        if convert:
            w_o = _Weight(wts["w_o_b"], j)
            w_up, w_down = _Weight(wts["w_up"], l), _Weight(wts["w_down"], l)
        else:
            w_o, w_up, w_down = (_Weight(w) for w in layer_bf16[l])
        fuse_q = pairs and j + 1 < n_b
        outs = _layer_call(x, (a, w_o), ln(l), w_up, w_down,
                           conv=False, stream_len=0, pairs=pairs, tm=tm, n_sub=n_sub,
                           w_q=_Weight(wts["w_q_bf16"], j + 1) if fuse_q else None,
                           q_scale=q_scale if fuse_q else 0.0, convert=convert)
        x = outs[0]
        if fuse_q:
            q = outs[1]
        elif j + 1 < n_b:
            (q,) = _proj_call(x, (_Weight(wts["w_q_bf16"], j + 1),), (q_scale,), (False,),
                              tm=proj_tm, keep=keep, pairs=pairs)
        converted.append(outs[len(outs) - 3:] if convert else [])
    return x, ztails, kf, vf, converted


def kernel(x_prompt, x_sample, cache_conv, cache_k, cache_v, ln_mix_g, ln_mix_b, ln_ffn_g, ln_ffn_b,
           w_up, w_down, w_in_a, conv_w_a, w_out_a, w_k, w_v, w_q_b, w_o_b, rel_bias_b):
    assert x_prompt.shape[0] == 1
    wts = dict(
        ln_mix_g=ln_mix_g, ln_mix_b=ln_mix_b, ln_ffn_g=ln_ffn_g, ln_ffn_b=ln_ffn_b,
        w_up=w_up, w_down=w_down, w_in_a=w_in_a, conv_w_a=conv_w_a, w_out_a=w_out_a, w_o_b=w_o_b,
        w_k_bf16=w_k.astype(BF16), w_v_bf16=w_v.astype(BF16), w_q_bf16=w_q_b.astype(BF16),
        rel_bias_b=rel_bias_b)

    nb, t = x_sample.shape[0], x_sample.shape[1]
    w = cache_k.shape[1]
    ns = nb * t
    ys, zts, kfs, vfs, layer_bf16 = _trunk(
        x_sample.reshape(ns, D_MODEL), cache_conv,
        (cache_k.reshape(nb, w, D_MODEL), cache_v.reshape(nb, w, D_MODEL)),
        wts, None, tm=ns, n_sub=1, proj_tm=ns, keep=ns)
    conv_sample = jnp.stack([zt.reshape(nb, t, D_MODEL)[:, t - 2:] for zt in zts])
    k_sample = kfs.reshape(nb, t, N_HEADS, HEAD_DIM)
    v_sample = vfs.reshape(nb, t, N_HEADS, HEAD_DIM)

    seq = x_prompt.shape[1]
    keep = min(BAND_PAST, seq)
    yp, ztp, kfp, vfp, _ = _trunk(x_prompt[0], None, None, wts, layer_bf16,
                                  tm=512, n_sub=2, proj_tm=1024, keep=keep)
    conv_prompt = jnp.stack([zt[SUBLANES_V7X - 2:] for zt in ztp])[:, None]
    k_prompt = kfp.reshape(1, keep, N_HEADS, HEAD_DIM)
    v_prompt = vfp.reshape(1, keep, N_HEADS, HEAD_DIM)
    return (yp[None], ys.reshape(nb, t, D_MODEL), conv_prompt, k_prompt, v_prompt,
            conv_sample, k_sample, v_sample)
```

```python
import dataclasses
import functools
import math

import jax
import jax.numpy as jnp
from jax import lax
from jax.experimental import pallas as pl
from jax.experimental.pallas import tpu as pltpu

D_MODEL = 1024
N_HEADS = 16
HEAD_DIM = 64
DEPTH = 4
N_A = 2
CHUNK = 64
N_PREV_CHUNKS = 8
BAND_PAST = N_PREV_CHUNKS * CHUNK
REL_MAX = 128
ALPHA = (2.0 * DEPTH) ** 0.25
LN_EPS = 1e-5
LOG2E = math.log2(math.e)

LANES_V7X = 128
SUBLANES_V7X = 8
VMEM_BYTES_V7X = 64 * 1024 * 1024
VMEM_LIMIT_BYTES = VMEM_BYTES_V7X - 8 * 1024 * 1024

PAIR_W = LANES_V7X
N_PAIRS = D_MODEL // PAIR_W
Q_BLOCK = 2 * CHUNK
K_BLOCK = BAND_PAST + Q_BLOCK
ATT_STEP = BAND_PAST
CONVERT_CHUNK_BYTES = 2 * 1024 * 1024
CONVERT_SLOTS = 4
REL_RING = 1024
MASK_VALUE = -1e30

BF16 = jnp.bfloat16
F32 = jnp.float32


def _const_spec(shape):
    zeros = (0,) * len(shape)
    return pl.BlockSpec(shape, lambda i: zeros, pipeline_mode=pl.Buffered(1))


@dataclasses.dataclass(frozen=True)
class _Weight:
    array: jax.Array
    layer: int | None = None

    @property
    def shape(self):
        return self.array.shape if self.layer is None else self.array.shape[1:]

    def spec(self):
        if self.layer is None:
            return _const_spec(self.array.shape)
        index = (self.layer,) + (0,) * len(self.shape)
        return pl.BlockSpec((None,) + self.shape, lambda i: index, pipeline_mode=pl.Buffered(1))


def _params():
    return pltpu.CompilerParams(dimension_semantics=("arbitrary",), vmem_limit_bytes=VMEM_LIMIT_BYTES)


def _dot(a, b):
    return jnp.dot(a, b, preferred_element_type=F32)


def _dot_nt(a, b):
    return lax.dot_general(a, b, (((1,), (1,)), ((), ())), preferred_element_type=F32)


def _layer_norm(x, g, b):
    mu = jnp.mean(x, axis=-1, keepdims=True)
    xc = x - mu
    var = jnp.mean(xc * xc, axis=-1, keepdims=True)
    return xc * lax.rsqrt(var + LN_EPS) * g + b


def _store_pairs(ref, val):
    for p in range(N_PAIRS):
        ref[p] = val[:, p * PAIR_W:(p + 1) * PAIR_W].astype(BF16)


@dataclasses.dataclass(frozen=True)
class _LayerCfg:
    conv: bool
    stream_len: int
    pairs: bool
    tm: int
    n_sub: int
    q_scale: float
    convert: tuple[int, int] | None = None

    @property
    def streams(self):
        return self.stream_len > 0


class _WeightConverter:
    def __init__(self, srcs, dsts, outs, stages, in_sems, out_sems):
        self.srcs, self.dsts, self.outs = srcs, dsts, outs
        self.stages, self.in_sems, self.out_sems = stages, in_sems, out_sems
        self.chunks = []
        per_pool = [0, 0]
        for j, dst in enumerate(dsts):
            pool = 0 if dst.shape[1] > D_MODEL else 1
            for r0 in range(0, dst.shape[0], stages[pool].shape[1]):
                self.chunks.append((pool, per_pool[pool], j, r0))
                per_pool[pool] += 1
        self.pos = 0
        self.started = 0
        self.done = [0, 0]
        self.writebacks = []

    def _copy(self, pool, k, j, r0):
        rows, cols = self.stages[pool].shape[1], self.dsts[j].shape[1]
        slot = k % CONVERT_SLOTS
        return pltpu.make_async_copy(self.srcs[j].at[pl.ds(r0, rows), :],
                                     self.stages[pool].at[slot, :, pl.ds(0, cols)],
                                     self.in_sems[pool].at[slot])

    def ready(self, j_last):
        chunks = self.chunks
        while self.pos < len(chunks) and chunks[self.pos][2] <= j_last:
            while (self.started < len(chunks)
                   and chunks[self.started][1] < self.done[chunks[self.started][0]] + CONVERT_SLOTS):
                self._copy(*chunks[self.started]).start()
                self.started += 1
            pool, k, j, r0 = chunks[self.pos]
            self._copy(pool, k, j, r0).wait()
            rows, cols = self.stages[pool].shape[1], self.dsts[j].shape[1]
            self.dsts[j][r0:r0 + rows, :] = self.stages[pool][k % CONVERT_SLOTS, :, :cols].astype(BF16)
            self.done[pool] += 1
            self.pos += 1
            if self.pos == len(chunks) or chunks[self.pos][2] != j:
                self.writebacks.append(
                    pltpu.make_async_copy(self.dsts[j], self.outs[j], self.out_sems.at[j]))
                self.writebacks[-1].start()

    def finish(self):
        self.ready(len(self.dsts) - 1)
        for writeback in self.writebacks:
            writeback.wait()


def _layer_kernel(*refs, cfg):
    it = iter(refs)
    x_ref = next(it)
    if cfg.conv:
        if cfg.streams:
            p1_ref, p2_ref = next(it), next(it)
        w_in_ref, cw_ref, w_mix_ref = next(it), next(it), next(it)
    else:
        a_ref, w_mix_ref = next(it), next(it)
    g1_ref, b1_ref, g2_ref, b2_ref = next(it), next(it), next(it), next(it)
    w_up_ref, w_down_ref = next(it), next(it)
    if cfg.q_scale:
        w_q_ref = next(it)
    y_ref = next(it)
    if cfg.q_scale:
        q_ref = next(it)
    if cfg.conv:
        ztail_ref = next(it)
    if cfg.convert:
        n_w = 4 if cfg.conv else 3
        outs = [next(it) for _ in range(n_w)]
    if cfg.conv and not cfg.streams:
        carry_ref = next(it)
    if cfg.convert:
        bufs = [next(it) for _ in range(n_w)]
        stages, in_sems, out_sems = (next(it), next(it)), (next(it), next(it)), next(it)
        mix_l, mlp_l = cfg.convert
        srcs = ([w_in_ref.at[mix_l]] if cfg.conv else []) + [
            w_mix_ref.at[mix_l], w_up_ref.at[mlp_l], w_down_ref.at[mlp_l]]
        converter = _WeightConverter(srcs, bufs, outs, stages, in_sems, out_sems)
        if cfg.conv:
            w_in_ref = bufs[0]
        w_mix_ref, w_up_ref, w_down_ref = bufs[-3:]

    def weight_ready(ref):
        if cfg.convert:
            converter.ready(next(j for j, buf in enumerate(bufs) if buf is ref))

    if cfg.conv and not cfg.streams:
        @pl.when(pl.program_id(0) == 0)
        def _():
            carry_ref[...] = jnp.zeros_like(carry_ref)
        prev = carry_ref[...]

    sm = cfg.tm // cfg.n_sub
    subs = range(cfg.n_sub)
    rows = [slice(sub * sm, (sub + 1) * sm) for sub in subs]
    x = [x_ref[r, :] for r in rows]
    if cfg.conv:
        weight_ready(w_in_ref)
        bch = [_dot(x[i].astype(BF16), w_in_ref[...]) for i in subs]
        row = lax.broadcasted_iota(jnp.int32, (sm, D_MODEL), 0)
        cw = cw_ref[...]
        mix_in = []
        for i in subs:
            b_gate = bch[i][:, :D_MODEL]
            z = bch[i][:, D_MODEL:2 * D_MODEL] * bch[i][:, 2 * D_MODEL:]
            r1 = pltpu.roll(z, 1, 0)
            r2 = pltpu.roll(z, 2, 0)
            if cfg.streams:
                rm = row & (cfg.stream_len - 1)
                zm1 = jnp.where(rm == 0, p1_ref[rows[i], :], r1)
                zm2 = jnp.where(rm < 2, p2_ref[rows[i], :], r2)
                ztail_ref[rows[i], :] = z
            else:
                pm1 = prev[SUBLANES_V7X - 1:SUBLANES_V7X, :]
                pm2 = prev[SUBLANES_V7X - 2:SUBLANES_V7X - 1, :]
                zm1 = jnp.where(row == 0, pm1, r1)
                zm2 = jnp.where(row == 0, pm2, jnp.where(row == 1, pm1, r2))
                prev = z[sm - SUBLANES_V7X:, :]
            yc = cw[0:1, :] * zm2 + cw[1:2, :] * zm1 + cw[2:3, :] * z
            mix_in.append((b_gate * yc).astype(BF16))
        if not cfg.streams:
            carry_ref[...] = prev
            ztail_ref[...] = prev
    elif cfg.pairs:
        mix_in = [jnp.concatenate([a_ref[p, r, :] for p in range(N_PAIRS)], axis=-1) for r in rows]
    else:
        mix_in = [a_ref[r, :] for r in rows]
    weight_ready(w_mix_ref)
    h = [_dot(mix_in[i], w_mix_ref[...]) for i in subs]
    x1 = [_layer_norm(ALPHA * x[i] + h[i], g1_ref[...], b1_ref[...]) for i in subs]
    weight_ready(w_up_ref)
    u = [_dot(x1[i].astype(BF16), w_up_ref[...]) for i in subs]
    u = [jnp.maximum(u[i], 0.0) for i in subs]
    weight_ready(w_down_ref)
    m = [_dot((u[i] * u[i]).astype(BF16), w_down_ref[...]) for i in subs]
    y = [_layer_norm(ALPHA * x1[i] + m[i], g2_ref[...], b2_ref[...]) for i in subs]
    for i in subs:
        y_ref[rows[i], :] = y[i]
    if cfg.q_scale:
        q = [_dot(y[i].astype(BF16), w_q_ref[...]) * cfg.q_scale for i in subs]
        for i in subs:
            for p in range(N_PAIRS):
                q_ref[p, rows[i], :] = q[i][:, p * PAIR_W:(p + 1) * PAIR_W].astype(BF16)
    if cfg.convert:
        converter.finish()


def _layer_call(x, mix_args, ln, w_up, w_down, *, conv, stream_len, pairs, tm, n_sub,
                w_q=None, q_scale=0.0, convert=False):
    s = x.shape[0]
    assert s % tm == 0 and tm % n_sub == 0
    assert stream_len & (stream_len - 1) == 0 and (stream_len == 0 or (tm // n_sub) % stream_len == 0)
    assert (w_q is None) == (q_scale == 0.0)
    n = s // tm
    streams = stream_len > 0
    row_spec = pl.BlockSpec((tm, D_MODEL), lambda i: (i, 0))
    pair_spec = pl.BlockSpec((N_PAIRS, tm, PAIR_W), lambda i: (0, i, 0))
    hbm_spec = pl.BlockSpec(memory_space=pl.ANY)
    w_spec = (lambda w: hbm_spec) if convert else (lambda w: w.spec())
    in_specs = [row_spec]
    args = [x]
    if conv:
        if streams:
            p1, p2, w_in, cw, w_mix = mix_args
            in_specs += [row_spec, row_spec]
            args += [p1, p2]
        else:
            w_in, cw, w_mix = mix_args
        in_specs += [w_spec(w_in), cw.spec(), w_spec(w_mix)]
        args += [w_in.array, cw.array, w_mix.array]
        converted = [w_in, w_mix, w_up, w_down]
    else:
        a, w_mix = mix_args
        in_specs += [pair_spec if pairs else row_spec, w_spec(w_mix)]
        args += [a, w_mix.array]
        converted = [w_mix, w_up, w_down]
    in_specs += [_const_spec((1, D_MODEL))] * 4 + [w_spec(w_up), w_spec(w_down)]
    args += list(ln) + [w_up.array, w_down.array]

    out_shape = [jax.ShapeDtypeStruct((s, D_MODEL), F32)]
    out_specs = [row_spec]
    if w_q is not None:
        in_specs.append(w_q.spec())
        args.append(w_q.array)
        out_shape.append(jax.ShapeDtypeStruct((N_PAIRS, s, PAIR_W), BF16))
        out_specs.append(pair_spec)
    scratch = []
    if conv:
        if streams:
            out_shape.append(jax.ShapeDtypeStruct((s, D_MODEL), F32))
            out_specs.append(row_spec)
        else:
            out_shape.append(jax.ShapeDtypeStruct((SUBLANES_V7X, D_MODEL), F32))
            out_specs.append(pl.BlockSpec((SUBLANES_V7X, D_MODEL), lambda i: (0, 0)))
            scratch.append(pltpu.VMEM((SUBLANES_V7X, D_MODEL), F32))
    if convert:
        assert n == 1 and all(w.layer is not None for w in converted)
        out_shape += [jax.ShapeDtypeStruct(w.shape, BF16) for w in converted]
        out_specs += [hbm_spec] * len(converted)
        scratch += [pltpu.VMEM(w.shape, BF16) for w in converted]
        wide = max(w.shape[1] for w in converted)
        chunk_rows = (CONVERT_CHUNK_BYTES // (4 * wide), CONVERT_CHUNK_BYTES // (4 * D_MODEL))
        assert all(w.shape[0] % chunk_rows[0 if w.shape[1] > D_MODEL else 1] == 0 for w in converted)
        scratch += [pltpu.VMEM((CONVERT_SLOTS, chunk_rows[0], wide), F32),
                    pltpu.VMEM((CONVERT_SLOTS, chunk_rows[1], D_MODEL), F32),
                    pltpu.SemaphoreType.DMA((CONVERT_SLOTS,)), pltpu.SemaphoreType.DMA((CONVERT_SLOTS,)),
                    pltpu.SemaphoreType.DMA((len(converted),))]
    cfg = _LayerCfg(conv=conv, stream_len=stream_len, pairs=pairs, tm=tm, n_sub=n_sub, q_scale=q_scale,
                    convert=(w_mix.layer, w_up.layer) if convert else None)
    return pl.pallas_call(
        functools.partial(_layer_kernel, cfg=cfg),
        grid=(n,),
        in_specs=in_specs,
        out_specs=out_specs,
        out_shape=out_shape,
        scratch_shapes=scratch,
        compiler_params=_params(),
        name="layer_conv" if conv else "layer_attn",
    )(*args)


def _proj_kernel(*refs, scales, tails, tail_rows, pairs):
    n = len(scales)
    x_ref = refs[0]
    w_refs = refs[1:1 + n]
    out_refs = list(refs[1 + n:])
    xb = x_ref[...].astype(BF16)
    tm = xb.shape[0]
    for j in range(n):
        pj = _dot(xb, w_refs[j][...])
        if scales[j] != 1.0:
            pj = pj * scales[j]
        if pairs:
            _store_pairs(out_refs.pop(0), pj)
        else:
            out_refs.pop(0)[...] = pj.astype(BF16)
        if tails[j]:
            out_refs.pop(0)[...] = pj[tm - tail_rows:, :]


def _proj_call(x, ws, scales, tails, *, tm, keep, pairs):
    s = x.shape[0]
    assert s % tm == 0
    n = s // tm
    tail_rows = min(tm, keep)
    assert keep % tail_rows == 0
    n_tail_blocks = keep // tail_rows
    row_spec = pl.BlockSpec((tm, D_MODEL), lambda i: (i, 0))
    pair_spec = pl.BlockSpec((N_PAIRS, tm, PAIR_W), lambda i: (0, i, 0))
    tail_spec = pl.BlockSpec((tail_rows, D_MODEL),
                             lambda i: (jnp.maximum(i - (n - n_tail_blocks), 0), 0))
    out_shape, out_specs = [], []
    for t in tails:
        if pairs:
            out_shape.append(jax.ShapeDtypeStruct((N_PAIRS, s, PAIR_W), BF16))
            out_specs.append(pair_spec)
        else:
            out_shape.append(jax.ShapeDtypeStruct((s, D_MODEL), BF16))
            out_specs.append(row_spec)
        if t:
            out_shape.append(jax.ShapeDtypeStruct((keep, D_MODEL), F32))
            out_specs.append(tail_spec)
    return pl.pallas_call(
        functools.partial(_proj_kernel, scales=tuple(scales), tails=tuple(tails),
                          tail_rows=tail_rows, pairs=pairs),
        grid=(n,),
        in_specs=[row_spec] + [w.spec() for w in ws],
        out_specs=out_specs,
        out_shape=out_shape,
        compiler_params=_params(),
        name="proj",
    )(x, *[w.array for w in ws])


def _rel_bias_rows(rvec_ref, h, rows, valid):
    row = jnp.broadcast_to(rvec_ref[h:h + 1, :], (rows, REL_RING))
    t = pltpu.roll(row, 0, 1, stride=1, stride_axis=0)
    return jnp.where(valid, t * LOG2E, MASK_VALUE)


def _softmax_weights(s):
    m = jnp.max(s, axis=-1, keepdims=True)
    e = jnp.exp2(s - m)
    return e.astype(BF16), jnp.sum(e, axis=-1, keepdims=True)


def _weighted_values(eb, l, v_parts):
    o = None
    off = 0
    for v in v_parts:
        part = _dot(eb[:, off:off + v.shape[0]], v)
        o = part if o is None else o + part
        off += v.shape[0]
    return o / l


def _attn_prompt_kernel(q_ref, kp_ref, kc_ref, vp_ref, vc_ref, rvec_ref, o_ref, bias_ref):
    step = pl.program_id(0)
    low = lax.broadcasted_iota(jnp.int32, (Q_BLOCK, PAIR_W), 1) < HEAD_DIM
    n_blocks = ATT_STEP // Q_BLOCK
    missing = [max(BAND_PAST - b * Q_BLOCK, 0) for b in range(n_blocks)]

    def build_bias(first_key):
        qi = lax.broadcasted_iota(jnp.int32, (Q_BLOCK, REL_RING), 0)
        kj = lax.broadcasted_iota(jnp.int32, (Q_BLOCK, REL_RING), 1)
        start = (qi >> 6) << 6
        valid = (kj >= start) & (kj < start + (BAND_PAST + CHUNK)) & (kj >= first_key)
        for h in range(N_HEADS):
            bias_ref[h * Q_BLOCK:(h + 1) * Q_BLOCK, :] = (
                _rel_bias_rows(rvec_ref, h, Q_BLOCK, valid)[:, :K_BLOCK])

    for blk in range(n_blocks):
        if blk == 0:
            rebuild = (step == 0) | (step == 1) if missing[-1] else (step == 0)
        else:
            rebuild = (step == 0) if missing[blk] != missing[blk - 1] else None
        if rebuild is not None:
            pl.when(rebuild)(functools.partial(
                build_bias, jnp.where(step == 0, missing[blk], 0)))

        q_rows = slice(blk * Q_BLOCK, (blk + 1) * Q_BLOCK)
        past_lo = blk * Q_BLOCK
        cur_rows = slice(max(past_lo - BAND_PAST, 0), past_lo + Q_BLOCK)

        def window(prev_ref, cur_ref, p):
            parts = [prev_ref[p, past_lo:, :]] if past_lo < BAND_PAST else []
            return parts + [cur_ref[p, cur_rows, :]]

        def scores(p):
            qb = q_ref[p, q_rows, :]
            qs = jnp.concatenate([jnp.where(low, qb, jnp.zeros_like(qb)),
                                  jnp.where(low, jnp.zeros_like(qb), qb)], axis=0)
            s = jnp.concatenate([_dot_nt(qs, k) for k in window(kp_ref, kc_ref, p)], axis=-1)
            return s + bias_ref[2 * p * Q_BLOCK:2 * (p + 1) * Q_BLOCK, :]

        def finish(p, s):
            o = _weighted_values(*_softmax_weights(s), window(vp_ref, vc_ref, p))
            o_ref[p, q_rows, :] = jnp.where(low, o[:Q_BLOCK], o[Q_BLOCK:]).astype(BF16)

        s_next = scores(0)
        for p in range(N_PAIRS):
            s_cur = s_next
            if p + 1 < N_PAIRS:
                s_next = scores(p + 1)
            finish(p, s_cur)


def _attn_prompt_call(q, k, v, rvec):
    s = q.shape[1]
    assert s % ATT_STEP == 0 and ATT_STEP % BAND_PAST == 0
    n = s // ATT_STEP
    ratio = ATT_STEP // BAND_PAST
    cur = pl.BlockSpec((N_PAIRS, ATT_STEP, PAIR_W), lambda i: (0, i, 0))
    prev = pl.BlockSpec((N_PAIRS, BAND_PAST, PAIR_W), lambda i: (0, jnp.maximum(ratio * i - 1, 0), 0))
    return pl.pallas_call(
        _attn_prompt_kernel,
        grid=(n,),
        in_specs=[cur, prev, cur, prev, cur, _const_spec(rvec.shape)],
        out_specs=cur,
        out_shape=jax.ShapeDtypeStruct((N_PAIRS, s, PAIR_W), BF16),
        scratch_shapes=[pltpu.VMEM((N_HEADS * Q_BLOCK, K_BLOCK), F32)],
        compiler_params=_params(),
        name="attn_prompt",
    )(q, k, k, v, v, rvec)


def _attn_decode_kernel(q_ref, kn_ref, vn_ref, kc_ref, vc_ref, rvec_ref, o_ref, bias_ref):
    t = q_ref.shape[0]
    w = kc_ref.shape[1]
    kpad = bias_ref.shape[1]

    @pl.when(pl.program_id(0) == 0)
    def _():
        kj = lax.broadcasted_iota(jnp.int32, (t, REL_RING), 1)
        valid = kj < w + t
        for h in range(N_HEADS):
            bias_ref[h * t:(h + 1) * t, :] = _rel_bias_rows(rvec_ref, h, t, valid)[:, :kpad]

    q = q_ref[...]
    head_of_lane = lax.broadcasted_iota(jnp.int32, (t, D_MODEL), 1) >> 6
    qs = jnp.concatenate(
        [jnp.where(head_of_lane == h, q, jnp.zeros_like(q)) for h in range(N_HEADS)], axis=0)
    zeros = jnp.zeros((kpad - w - t, D_MODEL), BF16)
    kk = jnp.concatenate([kc_ref[0].astype(BF16), kn_ref[...], zeros], axis=0)
    vv = jnp.concatenate([vc_ref[0].astype(BF16), vn_ref[...], zeros], axis=0)
    eb, l = _softmax_weights(_dot_nt(qs, kk) + bias_ref[...])
    o = _weighted_values(eb, l, (vv,))
    out = jnp.zeros((t, D_MODEL), F32)
    for h in range(N_HEADS):
        out = out + jnp.where(head_of_lane == h, o[h * t:(h + 1) * t, :], 0.0)
    o_ref[...] = out.astype(BF16)


def _attn_decode_call(q, kn, vn, cache_k, cache_v, rvec, *, t):
    nb, w = cache_k.shape[0], cache_k.shape[1]
    assert w == BAND_PAST
    kpad = -(-(w + t) // LANES_V7X) * LANES_V7X
    assert kpad + t <= REL_RING
    new_spec = pl.BlockSpec((t, D_MODEL), lambda i: (i, 0))
    cache_spec = pl.BlockSpec((1, w, D_MODEL), lambda i: (i, 0, 0))
    return pl.pallas_call(
        _attn_decode_kernel,
        grid=(nb,),
        in_specs=[new_spec, new_spec, new_spec, cache_spec, cache_spec, _const_spec(rvec.shape)],
        out_specs=new_spec,
        out_shape=jax.ShapeDtypeStruct((nb * t, D_MODEL), BF16),
        scratch_shapes=[pltpu.VMEM((N_HEADS * t, kpad), F32)],
        compiler_params=_params(),
        name="attn_decode",
    )(q, kn, vn, cache_k, cache_v, rvec)


def _rel_vector(table):
    m = jnp.arange(REL_RING)
    d = jnp.where(m < K_BLOCK, m, m - REL_RING)
    idx = jnp.clip(BAND_PAST - d, -REL_MAX, REL_MAX) + REL_MAX
    return table[idx].T.astype(F32)


def _trunk(x, conv_prev, caches, wts, layer_bf16, *, tm, n_sub, proj_tm, keep):
    s = x.shape[0]
    ln = lambda l: (wts["ln_mix_g"][l][None], wts["ln_mix_b"][l][None],
                    wts["ln_ffn_g"][l][None], wts["ln_ffn_b"][l][None])
    stream_len = 0 if conv_prev is None else s // conv_prev.shape[1]
    pairs = caches is None
    convert = layer_bf16 is None
    converted = []
    ztails = []
    for l in range(N_A):
        if convert:
            w_in, w_out = _Weight(wts["w_in_a"], l), _Weight(wts["w_out_a"], l)
            w_up, w_down = _Weight(wts["w_up"], l), _Weight(wts["w_down"], l)
        else:
            w_in, w_out, w_up, w_down = (_Weight(w) for w in layer_bf16[l])
        mix = (w_in, _Weight(wts["conv_w_a"], l), w_out)
        if stream_len:
            p2 = jnp.pad(conv_prev[l], ((0, 0), (0, stream_len - 2), (0, 0))).reshape(s, D_MODEL)
            p1 = jnp.pad(conv_prev[l][:, 1:], ((0, 0), (0, stream_len - 1), (0, 0))).reshape(s, D_MODEL)
            mix = (p1, p2) + mix
        x, zt, *bf16 = _layer_call(x, mix, ln(l), w_up, w_down, conv=True, stream_len=stream_len,
                                   pairs=False, tm=tm, n_sub=n_sub, convert=convert)
        ztails.append(zt)
        converted.append(bf16)

    q_scale = HEAD_DIM ** -0.5 * LOG2E
    q, k, kf, v, vf = _proj_call(
        x, (_Weight(wts["w_q_bf16"], 0), _Weight(wts["w_k_bf16"]), _Weight(wts["w_v_bf16"])),
        (q_scale, 1.0, 1.0), (False, True, True), tm=proj_tm, keep=keep, pairs=pairs)
    n_b = DEPTH - N_A
    for j in range(n_b):
        l = N_A + j
        rvec = _rel_vector(wts["rel_bias_b"][j])
        if caches is None:
            a = _attn_prompt_call(q, k, v, rvec)
        else:
            a = _attn_decode_call(q, k, v, caches[0], caches[1], rvec, t=s // caches[0].shape[0])
        if convert:
            w_o = _Weight(wts["w_o_b"], j)
            w_up, w_down = _Weight(wts["w_up"], l), _Weight(wts["w_down"], l)
        else:
            w_o, w_up, w_down = (_Weight(w) for w in layer_bf16[l])
        fuse_q = pairs and j + 1 < n_b
        outs = _layer_call(x, (a, w_o), ln(l), w_up, w_down,
                           conv=False, stream_len=0, pairs=pairs, tm=tm, n_sub=n_sub,
                           w_q=_Weight(wts["w_q_bf16"], j + 1) if fuse_q else None,
                           q_scale=q_scale if fuse_q else 0.0, convert=convert)
        x = outs[0]
        if fuse_q:
            q = outs[1]
        elif j + 1 < n_b:
            (q,) = _proj_call(x, (_Weight(wts["w_q_bf16"], j + 1),), (q_scale,), (False,),
                              tm=proj_tm, keep=keep, pairs=pairs)
        converted.append(outs[len(outs) - 3:] if convert else [])
    return x, ztails, kf, vf, converted


def kernel(x_prompt, x_sample, cache_conv, cache_k, cache_v, ln_mix_g, ln_mix_b, ln_ffn_g, ln_ffn_b,
           w_up, w_down, w_in_a, conv_w_a, w_out_a, w_k, w_v, w_q_b, w_o_b, rel_bias_b):
    assert x_prompt.shape[0] == 1
    wts = dict(
        ln_mix_g=ln_mix_g, ln_mix_b=ln_mix_b, ln_ffn_g=ln_ffn_g, ln_ffn_b=ln_ffn_b,
        w_up=w_up, w_down=w_down, w_in_a=w_in_a, conv_w_a=conv_w_a, w_out_a=w_out_a, w_o_b=w_o_b,
        w_k_bf16=w_k.astype(BF16), w_v_bf16=w_v.astype(BF16), w_q_bf16=w_q_b.astype(BF16),
        rel_bias_b=rel_bias_b)

    nb, t = x_sample.shape[0], x_sample.shape[1]
    w = cache_k.shape[1]
    ns = nb * t
    ys, zts, kfs, vfs, layer_bf16 = _trunk(
        x_sample.reshape(ns, D_MODEL), cache_conv,
        (cache_k.reshape(nb, w, D_MODEL), cache_v.reshape(nb, w, D_MODEL)),
        wts, None, tm=ns, n_sub=1, proj_tm=ns, keep=ns)
    conv_sample = jnp.stack([zt.reshape(nb, t, D_MODEL)[:, t - 2:] for zt in zts])
    k_sample = kfs.reshape(nb, t, N_HEADS, HEAD_DIM)
    v_sample = vfs.reshape(nb, t, N_HEADS, HEAD_DIM)

    seq = x_prompt.shape[1]
    keep = min(BAND_PAST, seq)
    yp, ztp, kfp, vfp, _ = _trunk(x_prompt[0], None, None, wts, layer_bf16,
                                  tm=512, n_sub=2, proj_tm=1024, keep=keep)
    conv_prompt = jnp.stack([zt[SUBLANES_V7X - 2:] for zt in ztp])[:, None]
    k_prompt = kfp.reshape(1, keep, N_HEADS, HEAD_DIM)
    v_prompt = vfp.reshape(1, keep, N_HEADS, HEAD_DIM)
    return (yp[None], ys.reshape(nb, t, D_MODEL), conv_prompt, k_prompt, v_prompt,
            conv_sample, k_sample, v_sample)
```

```python
import dataclasses
import functools
import math

import jax
import jax.numpy as jnp
from jax import lax
from jax.experimental import pallas as pl
from jax.experimental.pallas import tpu as pltpu

D_MODEL = 1024
N_HEADS = 16
HEAD_DIM = 64
DEPTH = 4
N_A = 2
CHUNK = 64
N_PREV_CHUNKS = 8
BAND_PAST = N_PREV_CHUNKS * CHUNK
REL_MAX = 128
ALPHA = (2.0 * DEPTH) ** 0.25
LN_EPS = 1e-5
LOG2E = math.log2(math.e)

LANES_V7X = 128
SUBLANES_V7X = 8
VMEM_BYTES_V7X = 64 * 1024 * 1024
VMEM_LIMIT_BYTES = VMEM_BYTES_V7X - 8 * 1024 * 1024

PAIR_W = LANES_V7X
N_PAIRS = D_MODEL // PAIR_W
Q_BLOCK = 2 * CHUNK
K_BLOCK = BAND_PAST + Q_BLOCK
ATT_STEP = BAND_PAST
LAYER_TILE_ROWS = 512
LAYER_SUB_TILES = 2
PROJ_TILE_ROWS = 1024
CONVERT_CHUNK_BYTES = 2 * 1024 * 1024
CONVERT_SLOTS = 4
REL_RING = 1024
MASK_VALUE = -1e30

BF16 = jnp.bfloat16
F32 = jnp.float32


def _const_spec(shape):
    zeros = (0,) * len(shape)
    return pl.BlockSpec(shape, lambda i: zeros, pipeline_mode=pl.Buffered(1))


@dataclasses.dataclass(frozen=True)
class _Weight:
    array: jax.Array
    layer: int | None = None

    @property
    def shape(self):
        return self.array.shape if self.layer is None else self.array.shape[1:]

    def spec(self):
        if self.layer is None:
            return _const_spec(self.array.shape)
        index = (self.layer,) + (0,) * len(self.shape)
        return pl.BlockSpec((None,) + self.shape, lambda i: index, pipeline_mode=pl.Buffered(1))


def _params():
    return pltpu.CompilerParams(dimension_semantics=("arbitrary",), vmem_limit_bytes=VMEM_LIMIT_BYTES)


def _dot(a, b):
    return jnp.dot(a, b, preferred_element_type=F32)


def _dot_nt(a, b):
    return lax.dot_general(a, b, (((1,), (1,)), ((), ())), preferred_element_type=F32)


def _layer_norm(x, g, b):
    mu = jnp.mean(x, axis=-1, keepdims=True)
    xc = x - mu
    var = jnp.mean(xc * xc, axis=-1, keepdims=True)
    return xc * lax.rsqrt(var + LN_EPS) * g + b


def _store_pairs(ref, val):
    for p in range(N_PAIRS):
        ref[p] = val[:, p * PAIR_W:(p + 1) * PAIR_W].astype(BF16)


@dataclasses.dataclass(frozen=True)
class _LayerCfg:
    conv: bool
    stream_len: int
    pairs: bool
    tm: int
    n_sub: int
    q_scale: float
    convert: tuple[int, int] | None = None

    @property
    def streams(self):
        return self.stream_len > 0


class _WeightConverter:
    def __init__(self, srcs, dsts, outs, stages, in_sems, out_sems):
        self.srcs, self.dsts, self.outs = srcs, dsts, outs
        self.stages, self.in_sems, self.out_sems = stages, in_sems, out_sems
        self.chunks = []
        per_pool = [0, 0]
        for j, dst in enumerate(dsts):
            pool = 0 if dst.shape[1] > D_MODEL else 1
            for r0 in range(0, dst.shape[0], stages[pool].shape[1]):
                self.chunks.append((pool, per_pool[pool], j, r0))
                per_pool[pool] += 1
        self.pos = 0
        self.started = 0
        self.done = [0, 0]
        self.writebacks = []

    def _copy(self, pool, k, j, r0):
        rows, cols = self.stages[pool].shape[1], self.dsts[j].shape[1]
        slot = k % CONVERT_SLOTS
        return pltpu.make_async_copy(self.srcs[j].at[pl.ds(r0, rows), :],
                                     self.stages[pool].at[slot, :, pl.ds(0, cols)],
                                     self.in_sems[pool].at[slot])

    def ready(self, j_last):
        chunks = self.chunks
        while self.pos < len(chunks) and chunks[self.pos][2] <= j_last:
            while (self.started < len(chunks)
                   and chunks[self.started][1] < self.done[chunks[self.started][0]] + CONVERT_SLOTS):
                self._copy(*chunks[self.started]).start()
                self.started += 1
            pool, k, j, r0 = chunks[self.pos]
            self._copy(pool, k, j, r0).wait()
            rows, cols = self.stages[pool].shape[1], self.dsts[j].shape[1]
            self.dsts[j][r0:r0 + rows, :] = self.stages[pool][k % CONVERT_SLOTS, :, :cols].astype(BF16)
            self.done[pool] += 1
            self.pos += 1
            if self.pos == len(chunks) or chunks[self.pos][2] != j:
                self.writebacks.append(
                    pltpu.make_async_copy(self.dsts[j], self.outs[j], self.out_sems.at[j]))
                self.writebacks[-1].start()

    def finish(self):
        self.ready(len(self.dsts) - 1)
        for writeback in self.writebacks:
            writeback.wait()


def _layer_kernel(*refs, cfg):
    it = iter(refs)
    x_ref = next(it)
    if cfg.conv:
        if cfg.streams:
            p1_ref, p2_ref = next(it), next(it)
        w_in_ref, cw_ref, w_mix_ref = next(it), next(it), next(it)
    else:
        a_ref, w_mix_ref = next(it), next(it)
    g1_ref, b1_ref, g2_ref, b2_ref = next(it), next(it), next(it), next(it)
    w_up_ref, w_down_ref = next(it), next(it)
    if cfg.q_scale:
        w_q_ref = next(it)
    y_ref = next(it)
    if cfg.q_scale:
        q_ref = next(it)
    if cfg.conv:
        ztail_ref = next(it)
    if cfg.convert:
        n_w = 4 if cfg.conv else 3
        outs = [next(it) for _ in range(n_w)]
    if cfg.conv and not cfg.streams:
        carry_ref = next(it)
    if cfg.convert:
        bufs = [next(it) for _ in range(n_w)]
        stages, in_sems, out_sems = (next(it), next(it)), (next(it), next(it)), next(it)
        mix_l, mlp_l = cfg.convert
        srcs = ([w_in_ref.at[mix_l]] if cfg.conv else []) + [
            w_mix_ref.at[mix_l], w_up_ref.at[mlp_l], w_down_ref.at[mlp_l]]
        converter = _WeightConverter(srcs, bufs, outs, stages, in_sems, out_sems)
        if cfg.conv:
            w_in_ref = bufs[0]
        w_mix_ref, w_up_ref, w_down_ref = bufs[-3:]

    def weight_ready(ref):
        if cfg.convert:
            converter.ready(next(j for j, buf in enumerate(bufs) if buf is ref))

    if cfg.conv and not cfg.streams:
        @pl.when(pl.program_id(0) == 0)
        def _():
            carry_ref[...] = jnp.zeros_like(carry_ref)
        prev = carry_ref[...]

    sm = cfg.tm // cfg.n_sub
    subs = range(cfg.n_sub)
    rows = [slice(sub * sm, (sub + 1) * sm) for sub in subs]
    x = [x_ref[r, :] for r in rows]
    if cfg.conv:
        weight_ready(w_in_ref)
        bch = [_dot(x[i].astype(BF16), w_in_ref[...]) for i in subs]
        row = lax.broadcasted_iota(jnp.int32, (sm, D_MODEL), 0)
        cw = cw_ref[...]
        mix_in = []
        for i in subs:
            b_gate = bch[i][:, :D_MODEL]
            z = bch[i][:, D_MODEL:2 * D_MODEL] * bch[i][:, 2 * D_MODEL:]
            r1 = pltpu.roll(z, 1, 0)
            r2 = pltpu.roll(z, 2, 0)
            if cfg.streams:
                rm = row & (cfg.stream_len - 1)
                zm1 = jnp.where(rm == 0, p1_ref[rows[i], :], r1)
                zm2 = jnp.where(rm < 2, p2_ref[rows[i], :], r2)
                ztail_ref[rows[i], :] = z
            else:
                pm1 = prev[SUBLANES_V7X - 1:SUBLANES_V7X, :]
                pm2 = prev[SUBLANES_V7X - 2:SUBLANES_V7X - 1, :]
                zm1 = jnp.where(row == 0, pm1, r1)
                zm2 = jnp.where(row == 0, pm2, jnp.where(row == 1, pm1, r2))
                prev = z[sm - SUBLANES_V7X:, :]
            yc = cw[0:1, :] * zm2 + cw[1:2, :] * zm1 + cw[2:3, :] * z
            mix_in.append((b_gate * yc).astype(BF16))
        if not cfg.streams:
            carry_ref[...] = prev
            ztail_ref[...] = prev
    elif cfg.pairs:
        mix_in = [jnp.concatenate([a_ref[p, r, :] for p in range(N_PAIRS)], axis=-1) for r in rows]
    else:
        mix_in = [a_ref[r, :] for r in rows]
    weight_ready(w_mix_ref)
    h = [_dot(mix_in[i], w_mix_ref[...]) for i in subs]
    x1 = [_layer_norm(ALPHA * x[i] + h[i], g1_ref[...], b1_ref[...]) for i in subs]
    weight_ready(w_up_ref)
    u = [_dot(x1[i].astype(BF16), w_up_ref[...]) for i in subs]
    u = [jnp.maximum(u[i], 0.0) for i in subs]
    weight_ready(w_down_ref)
    m = [_dot((u[i] * u[i]).astype(BF16), w_down_ref[...]) for i in subs]
    y = [_layer_norm(ALPHA * x1[i] + m[i], g2_ref[...], b2_ref[...]) for i in subs]
    for i in subs:
        y_ref[rows[i], :] = y[i]
    if cfg.q_scale:
        q = [_dot(y[i].astype(BF16), w_q_ref[...]) * cfg.q_scale for i in subs]
        for i in subs:
            for p in range(N_PAIRS):
                q_ref[p, rows[i], :] = q[i][:, p * PAIR_W:(p + 1) * PAIR_W].astype(BF16)
    if cfg.convert:
        converter.finish()


def _layer_call(x, mix_args, ln, w_up, w_down, *, conv, stream_len, pairs, tm, n_sub,
                w_q=None, q_scale=0.0, convert=False):
    s = x.shape[0]
    assert s % tm == 0 and tm % n_sub == 0
    assert stream_len & (stream_len - 1) == 0 and (stream_len == 0 or (tm // n_sub) % stream_len == 0)
    assert (w_q is None) == (q_scale == 0.0)
    n = s // tm
    streams = stream_len > 0
    row_spec = pl.BlockSpec((tm, D_MODEL), lambda i: (i, 0))
    pair_spec = pl.BlockSpec((N_PAIRS, tm, PAIR_W), lambda i: (0, i, 0))
    hbm_spec = pl.BlockSpec(memory_space=pl.ANY)
    w_spec = (lambda w: hbm_spec) if convert else (lambda w: w.spec())
    in_specs = [row_spec]
    args = [x]
    if conv:
        if streams:
            p1, p2, w_in, cw, w_mix = mix_args
            in_specs += [row_spec, row_spec]
            args += [p1, p2]
        else:
            w_in, cw, w_mix = mix_args
        in_specs += [w_spec(w_in), cw.spec(), w_spec(w_mix)]
        args += [w_in.array, cw.array, w_mix.array]
        converted = [w_in, w_mix, w_up, w_down]
    else:
        a, w_mix = mix_args
        in_specs += [pair_spec if pairs else row_spec, w_spec(w_mix)]
        args += [a, w_mix.array]
        converted = [w_mix, w_up, w_down]
    in_specs += [_const_spec((1, D_MODEL))] * 4 + [w_spec(w_up), w_spec(w_down)]
    args += list(ln) + [w_up.array, w_down.array]

    out_shape = [jax.ShapeDtypeStruct((s, D_MODEL), F32)]
    out_specs = [row_spec]
    if w_q is not None:
        in_specs.append(w_q.spec())
        args.append(w_q.array)
        out_shape.append(jax.ShapeDtypeStruct((N_PAIRS, s, PAIR_W), BF16))
        out_specs.append(pair_spec)
    scratch = []
    if conv:
        if streams:
            out_shape.append(jax.ShapeDtypeStruct((s, D_MODEL), F32))
            out_specs.append(row_spec)
        else:
            out_shape.append(jax.ShapeDtypeStruct((SUBLANES_V7X, D_MODEL), F32))
            out_specs.append(pl.BlockSpec((SUBLANES_V7X, D_MODEL), lambda i: (0, 0)))
            scratch.append(pltpu.VMEM((SUBLANES_V7X, D_MODEL), F32))
    if convert:
        assert n == 1 and all(w.layer is not None for w in converted)
        out_shape += [jax.ShapeDtypeStruct(w.shape, BF16) for w in converted]
        out_specs += [hbm_spec] * len(converted)
        scratch += [pltpu.VMEM(w.shape, BF16) for w in converted]
        wide = max(w.shape[1] for w in converted)
        chunk_rows = (CONVERT_CHUNK_BYTES // (4 * wide), CONVERT_CHUNK_BYTES // (4 * D_MODEL))
        assert all(w.shape[0] % chunk_rows[0 if w.shape[1] > D_MODEL else 1] == 0 for w in converted)
        scratch += [pltpu.VMEM((CONVERT_SLOTS, chunk_rows[0], wide), F32),
                    pltpu.VMEM((CONVERT_SLOTS, chunk_rows[1], D_MODEL), F32),
                    pltpu.SemaphoreType.DMA((CONVERT_SLOTS,)), pltpu.SemaphoreType.DMA((CONVERT_SLOTS,)),
                    pltpu.SemaphoreType.DMA((len(converted),))]
    cfg = _LayerCfg(conv=conv, stream_len=stream_len, pairs=pairs, tm=tm, n_sub=n_sub, q_scale=q_scale,
                    convert=(w_mix.layer, w_up.layer) if convert else None)
    return pl.pallas_call(
        functools.partial(_layer_kernel, cfg=cfg),
        grid=(n,),
        in_specs=in_specs,
        out_specs=out_specs,
        out_shape=out_shape,
        scratch_shapes=scratch,
        compiler_params=_params(),
        name="layer_conv" if conv else "layer_attn",
    )(*args)


def _proj_kernel(*refs, scales, tails, tail_rows, pairs):
    n = len(scales)
    x_ref = refs[0]
    w_refs = refs[1:1 + n]
    out_refs = list(refs[1 + n:])
    xb = x_ref[...].astype(BF16)
    tm = xb.shape[0]
    for j in range(n):
        pj = _dot(xb, w_refs[j][...])
        if scales[j] != 1.0:
            pj = pj * scales[j]
        if pairs:
            _store_pairs(out_refs.pop(0), pj)
        else:
            out_refs.pop(0)[...] = pj.astype(BF16)
        if tails[j]:
            out_refs.pop(0)[...] = pj[tm - tail_rows:, :]


def _proj_call(x, ws, scales, tails, *, tm, keep, pairs):
    s = x.shape[0]
    assert s % tm == 0
    n = s // tm
    tail_rows = min(tm, keep)
    assert keep % tail_rows == 0
    n_tail_blocks = keep // tail_rows
    row_spec = pl.BlockSpec((tm, D_MODEL), lambda i: (i, 0))
    pair_spec = pl.BlockSpec((N_PAIRS, tm, PAIR_W), lambda i: (0, i, 0))
    tail_spec = pl.BlockSpec((tail_rows, D_MODEL),
                             lambda i: (jnp.maximum(i - (n - n_tail_blocks), 0), 0))
    out_shape, out_specs = [], []
    for t in tails:
        if pairs:
            out_shape.append(jax.ShapeDtypeStruct((N_PAIRS, s, PAIR_W), BF16))
            out_specs.append(pair_spec)
        else:
            out_shape.append(jax.ShapeDtypeStruct((s, D_MODEL), BF16))
            out_specs.append(row_spec)
        if t:
            out_shape.append(jax.ShapeDtypeStruct((keep, D_MODEL), F32))
            out_specs.append(tail_spec)
    return pl.pallas_call(
        functools.partial(_proj_kernel, scales=tuple(scales), tails=tuple(tails),
                          tail_rows=tail_rows, pairs=pairs),
        grid=(n,),
        in_specs=[row_spec] + [w.spec() for w in ws],
        out_specs=out_specs,
        out_shape=out_shape,
        compiler_params=_params(),
        name="proj",
    )(x, *[w.array for w in ws])


def _rel_bias_rows(rvec_ref, h, rows, valid):
    row = jnp.broadcast_to(rvec_ref[h:h + 1, :], (rows, REL_RING))
    t = pltpu.roll(row, 0, 1, stride=1, stride_axis=0)
    return jnp.where(valid, t * LOG2E, MASK_VALUE)


def _softmax_weights(s):
    m = jnp.max(s, axis=-1, keepdims=True)
    e = jnp.exp2(s - m)
    return e.astype(BF16), jnp.sum(e, axis=-1, keepdims=True)


def _weighted_values(eb, l, v_parts):
    o = None
    off = 0
    for v in v_parts:
        part = _dot(eb[:, off:off + v.shape[0]], v)
        o = part if o is None else o + part
        off += v.shape[0]
    return o / l


def _attn_prompt_kernel(q_ref, kp_ref, kc_ref, vp_ref, vc_ref, rvec_ref, o_ref, bias_ref):
    step = pl.program_id(0)
    low = lax.broadcasted_iota(jnp.int32, (Q_BLOCK, PAIR_W), 1) < HEAD_DIM
    n_blocks = ATT_STEP // Q_BLOCK
    missing = [max(BAND_PAST - b * Q_BLOCK, 0) for b in range(n_blocks)]

    def build_bias(first_key):
        qi = lax.broadcasted_iota(jnp.int32, (Q_BLOCK, REL_RING), 0)
        kj = lax.broadcasted_iota(jnp.int32, (Q_BLOCK, REL_RING), 1)
        start = (qi >> 6) << 6
        valid = (kj >= start) & (kj < start + (BAND_PAST + CHUNK)) & (kj >= first_key)
        for h in range(N_HEADS):
            bias_ref[h * Q_BLOCK:(h + 1) * Q_BLOCK, :] = (
                _rel_bias_rows(rvec_ref, h, Q_BLOCK, valid)[:, :K_BLOCK])

    for blk in range(n_blocks):
        if blk == 0:
            rebuild = (step == 0) | (step == 1) if missing[-1] else (step == 0)
        else:
            rebuild = (step == 0) if missing[blk] != missing[blk - 1] else None
        if rebuild is not None:
            pl.when(rebuild)(functools.partial(
                build_bias, jnp.where(step == 0, missing[blk], 0)))

        q_rows = slice(blk * Q_BLOCK, (blk + 1) * Q_BLOCK)
        past_lo = blk * Q_BLOCK
        cur_rows = slice(max(past_lo - BAND_PAST, 0), past_lo + Q_BLOCK)

        def window(prev_ref, cur_ref, p):
            parts = [prev_ref[p, past_lo:, :]] if past_lo < BAND_PAST else []
            return parts + [cur_ref[p, cur_rows, :]]

        def scores(p):
            qb = q_ref[p, q_rows, :]
            qs = jnp.concatenate([jnp.where(low, qb, jnp.zeros_like(qb)),
                                  jnp.where(low, jnp.zeros_like(qb), qb)], axis=0)
            s = jnp.concatenate([_dot_nt(qs, k) for k in window(kp_ref, kc_ref, p)], axis=-1)
            return s + bias_ref[2 * p * Q_BLOCK:2 * (p + 1) * Q_BLOCK, :]

        def finish(p, s):
            o = _weighted_values(*_softmax_weights(s), window(vp_ref, vc_ref, p))
            o_ref[p, q_rows, :] = jnp.where(low, o[:Q_BLOCK], o[Q_BLOCK:]).astype(BF16)

        s_next = scores(0)
        for p in range(N_PAIRS):
            s_cur = s_next
            if p + 1 < N_PAIRS:
                s_next = scores(p + 1)
            finish(p, s_cur)


def _attn_prompt_call(q, k, v, rvec):
    s = q.shape[1]
    assert s % ATT_STEP == 0 and ATT_STEP % BAND_PAST == 0
    n = s // ATT_STEP
    ratio = ATT_STEP // BAND_PAST
    cur = pl.BlockSpec((N_PAIRS, ATT_STEP, PAIR_W), lambda i: (0, i, 0))
    prev = pl.BlockSpec((N_PAIRS, BAND_PAST, PAIR_W), lambda i: (0, jnp.maximum(ratio * i - 1, 0), 0))
    return pl.pallas_call(
        _attn_prompt_kernel,
        grid=(n,),
        in_specs=[cur, prev, cur, prev, cur, _const_spec(rvec.shape)],
        out_specs=cur,
        out_shape=jax.ShapeDtypeStruct((N_PAIRS, s, PAIR_W), BF16),
        scratch_shapes=[pltpu.VMEM((N_HEADS * Q_BLOCK, K_BLOCK), F32)],
        compiler_params=_params(),
        name="attn_prompt",
    )(q, k, k, v, v, rvec)


def _attn_decode_kernel(q_ref, kn_ref, vn_ref, kc_ref, vc_ref, rvec_ref, o_ref, bias_ref):
    t = q_ref.shape[0]
    w = kc_ref.shape[1]
    kpad = bias_ref.shape[1]

    @pl.when(pl.program_id(0) == 0)
    def _():
        kj = lax.broadcasted_iota(jnp.int32, (t, REL_RING), 1)
        valid = kj < w + t
        for h in range(N_HEADS):
            bias_ref[h * t:(h + 1) * t, :] = _rel_bias_rows(rvec_ref, h, t, valid)[:, :kpad]

    q = q_ref[...]
    head_of_lane = lax.broadcasted_iota(jnp.int32, (t, D_MODEL), 1) >> 6
    qs = jnp.concatenate(
        [jnp.where(head_of_lane == h, q, jnp.zeros_like(q)) for h in range(N_HEADS)], axis=0)
    zeros = jnp.zeros((kpad - w - t, D_MODEL), BF16)
    kk = jnp.concatenate([kc_ref[0].astype(BF16), kn_ref[...], zeros], axis=0)
    vv = jnp.concatenate([vc_ref[0].astype(BF16), vn_ref[...], zeros], axis=0)
    eb, l = _softmax_weights(_dot_nt(qs, kk) + bias_ref[...])
    o = _weighted_values(eb, l, (vv,))
    out = jnp.zeros((t, D_MODEL), F32)
    for h in range(N_HEADS):
        out = out + jnp.where(head_of_lane == h, o[h * t:(h + 1) * t, :], 0.0)
    o_ref[...] = out.astype(BF16)


def _attn_decode_call(q, kn, vn, cache_k, cache_v, rvec, *, t):
    nb, w = cache_k.shape[0], cache_k.shape[1]
    assert w == BAND_PAST
    kpad = -(-(w + t) // LANES_V7X) * LANES_V7X
    assert kpad + t <= REL_RING
    new_spec = pl.BlockSpec((t, D_MODEL), lambda i: (i, 0))
    cache_spec = pl.BlockSpec((1, w, D_MODEL), lambda i: (i, 0, 0))
    return pl.pallas_call(
        _attn_decode_kernel,
        grid=(nb,),
        in_specs=[new_spec, new_spec, new_spec, cache_spec, cache_spec, _const_spec(rvec.shape)],
        out_specs=new_spec,
        out_shape=jax.ShapeDtypeStruct((nb * t, D_MODEL), BF16),
        scratch_shapes=[pltpu.VMEM((N_HEADS * t, kpad), F32)],
        compiler_params=_params(),
        name="attn_decode",
    )(q, kn, vn, cache_k, cache_v, rvec)


def _rel_vector(table):
    m = jnp.arange(REL_RING)
    d = jnp.where(m < K_BLOCK, m, m - REL_RING)
    idx = jnp.clip(BAND_PAST - d, -REL_MAX, REL_MAX) + REL_MAX
    return table[idx].T.astype(F32)


def _trunk(x, conv_prev, caches, wts, layer_bf16, *, tm, n_sub, proj_tm, keep):
    s = x.shape[0]
    ln = lambda l: (wts["ln_mix_g"][l][None], wts["ln_mix_b"][l][None],
                    wts["ln_ffn_g"][l][None], wts["ln_ffn_b"][l][None])
    stream_len = 0 if conv_prev is None else s // conv_prev.shape[1]
    pairs = caches is None
    convert = layer_bf16 is None
    converted = []
    ztails = []
    for l in range(N_A):
        if convert:
            w_in, w_out = _Weight(wts["w_in_a"], l), _Weight(wts["w_out_a"], l)
            w_up, w_down = _Weight(wts["w_up"], l), _Weight(wts["w_down"], l)
        else:
            w_in, w_out, w_up, w_down = (_Weight(w) for w in layer_bf16[l])
        mix = (w_in, _Weight(wts["conv_w_a"], l), w_out)
        if stream_len:
            p2 = jnp.pad(conv_prev[l], ((0, 0), (0, stream_len - 2), (0, 0))).reshape(s, D_MODEL)
            p1 = jnp.pad(conv_prev[l][:, 1:], ((0, 0), (0, stream_len - 1), (0, 0))).reshape(s, D_MODEL)
            mix = (p1, p2) + mix
        x, zt, *bf16 = _layer_call(x, mix, ln(l), w_up, w_down, conv=True, stream_len=stream_len,
                                   pairs=False, tm=tm, n_sub=n_sub, convert=convert)
        ztails.append(zt)
        converted.append(bf16)

    q_scale = HEAD_DIM ** -0.5 * LOG2E
    q, k, kf, v, vf = _proj_call(
        x, (_Weight(wts["w_q_bf16"], 0), _Weight(wts["w_k_bf16"]), _Weight(wts["w_v_bf16"])),
        (q_scale, 1.0, 1.0), (False, True, True), tm=proj_tm, keep=keep, pairs=pairs)
    n_b = DEPTH - N_A
    for j in range(n_b):
        l = N_A + j
        rvec = _rel_vector(wts["rel_bias_b"][j])
        if caches is None:
            a = _attn_prompt_call(q, k, v, rvec)
        else:
            a = _attn_decode_call(q, k, v, caches[0], caches[1], rvec, t=s // caches[0].shape[0])
        if convert:
            w_o = _Weight(wts["w_o_b"], j)
            w_up, w_down = _Weight(wts["w_up"], l), _Weight(wts["w_down"], l)
        else:
            w_o, w_up, w_down = (_Weight(w) for w in layer_bf16[l])
        fuse_q = pairs and j + 1 < n_b
        outs = _layer_call(x, (a, w_o), ln(l), w_up, w_down,
                           conv=False, stream_len=0, pairs=pairs, tm=tm, n_sub=n_sub,
                           w_q=_Weight(wts["w_q_bf16"], j + 1) if fuse_q else None,
                           q_scale=q_scale if fuse_q else 0.0, convert=convert)
        x = outs[0]
        if fuse_q:
            q = outs[1]
        elif j + 1 < n_b:
            (q,) = _proj_call(x, (_Weight(wts["w_q_bf16"], j + 1),), (q_scale,), (False,),
                              tm=proj_tm, keep=keep, pairs=pairs)
        converted.append(outs[len(outs) - 3:] if convert else [])
    return x, ztails, kf, vf, converted


def kernel(x_prompt, x_sample, cache_conv, cache_k, cache_v, ln_mix_g, ln_mix_b, ln_ffn_g, ln_ffn_b,
           w_up, w_down, w_in_a, conv_w_a, w_out_a, w_k, w_v, w_q_b, w_o_b, rel_bias_b):
    assert x_prompt.shape[0] == 1
    wts = dict(
        ln_mix_g=ln_mix_g, ln_mix_b=ln_mix_b, ln_ffn_g=ln_ffn_g, ln_ffn_b=ln_ffn_b,
        w_up=w_up, w_down=w_down, w_in_a=w_in_a, conv_w_a=conv_w_a, w_out_a=w_out_a, w_o_b=w_o_b,
        w_k_bf16=w_k.astype(BF16), w_v_bf16=w_v.astype(BF16), w_q_bf16=w_q_b.astype(BF16),
        rel_bias_b=rel_bias_b)

    nb, t = x_sample.shape[0], x_sample.shape[1]
    w = cache_k.shape[1]
    ns = nb * t
    ys, zts, kfs, vfs, layer_bf16 = _trunk(
        x_sample.reshape(ns, D_MODEL), cache_conv,
        (cache_k.reshape(nb, w, D_MODEL), cache_v.reshape(nb, w, D_MODEL)),
        wts, None, tm=ns, n_sub=1, proj_tm=ns, keep=ns)
    conv_sample = jnp.stack([zt.reshape(nb, t, D_MODEL)[:, t - 2:] for zt in zts])
    k_sample = kfs.reshape(nb, t, N_HEADS, HEAD_DIM)
    v_sample = vfs.reshape(nb, t, N_HEADS, HEAD_DIM)

    seq = x_prompt.shape[1]
    keep = min(BAND_PAST, seq)
    yp, ztp, kfp, vfp, _ = _trunk(x_prompt[0], None, None, wts, layer_bf16,
                                  tm=LAYER_TILE_ROWS, n_sub=LAYER_SUB_TILES, proj_tm=PROJ_TILE_ROWS,
                                  keep=keep)
    conv_prompt = jnp.stack([zt[SUBLANES_V7X - 2:] for zt in ztp])[:, None]
    k_prompt = kfp.reshape(1, keep, N_HEADS, HEAD_DIM)
    v_prompt = vfp.reshape(1, keep, N_HEADS, HEAD_DIM)
    return (yp[None], ys.reshape(nb, t, D_MODEL), conv_prompt, k_prompt, v_prompt,
            conv_sample, k_sample, v_sample)
```

```python
import dataclasses
import functools
import math

import jax
import jax.numpy as jnp
from jax import lax
from jax.experimental import pallas as pl
from jax.experimental.pallas import tpu as pltpu

D_MODEL = 1024
N_HEADS = 16
HEAD_DIM = 64
DEPTH = 4
N_A = 2
CHUNK = 64
N_PREV_CHUNKS = 8
BAND_PAST = N_PREV_CHUNKS * CHUNK
REL_MAX = 128
ALPHA = (2.0 * DEPTH) ** 0.25
LN_EPS = 1e-5
LOG2E = math.log2(math.e)

LANES_V7X = 128
SUBLANES_V7X = 8
VMEM_BYTES_V7X = 64 * 1024 * 1024
VMEM_LIMIT_BYTES = VMEM_BYTES_V7X - 8 * 1024 * 1024

PAIR_W = LANES_V7X
N_PAIRS = D_MODEL // PAIR_W
Q_BLOCK = 2 * CHUNK
K_BLOCK = BAND_PAST + Q_BLOCK
ATT_STEP = 2 * BAND_PAST
LAYER_TILE_ROWS = 512
LAYER_SUB_TILES = 2
PROJ_TILE_ROWS = 1024
CONVERT_CHUNK_BYTES = 2 * 1024 * 1024
CONVERT_SLOTS = 4
REL_RING = 1024
MASK_VALUE = -1e30

BF16 = jnp.bfloat16
F32 = jnp.float32


def _const_spec(shape):
    zeros = (0,) * len(shape)
    return pl.BlockSpec(shape, lambda i: zeros, pipeline_mode=pl.Buffered(1))


@dataclasses.dataclass(frozen=True)
class _Weight:
    array: jax.Array
    layer: int | None = None

    @property
    def shape(self):
        return self.array.shape if self.layer is None else self.array.shape[1:]

    def spec(self):
        if self.layer is None:
            return _const_spec(self.array.shape)
        index = (self.layer,) + (0,) * len(self.shape)
        return pl.BlockSpec((None,) + self.shape, lambda i: index, pipeline_mode=pl.Buffered(1))


def _params():
    return pltpu.CompilerParams(dimension_semantics=("arbitrary",), vmem_limit_bytes=VMEM_LIMIT_BYTES)


def _dot(a, b):
    return jnp.dot(a, b, preferred_element_type=F32)


def _dot_nt(a, b):
    return lax.dot_general(a, b, (((1,), (1,)), ((), ())), preferred_element_type=F32)


def _layer_norm(x, g, b):
    mu = jnp.mean(x, axis=-1, keepdims=True)
    xc = x - mu
    var = jnp.mean(xc * xc, axis=-1, keepdims=True)
    return xc * lax.rsqrt(var + LN_EPS) * g + b


def _store_pairs(ref, val):
    for p in range(N_PAIRS):
        ref[p] = val[:, p * PAIR_W:(p + 1) * PAIR_W].astype(BF16)


@dataclasses.dataclass(frozen=True)
class _LayerCfg:
    conv: bool
    stream_len: int
    pairs: bool
    tm: int
    n_sub: int
    q_scale: float
    convert: tuple[int, int] | None = None

    @property
    def streams(self):
        return self.stream_len > 0


class _WeightConverter:
    def __init__(self, srcs, dsts, outs, stages, in_sems, out_sems):
        self.srcs, self.dsts, self.outs = srcs, dsts, outs
        self.stages, self.in_sems, self.out_sems = stages, in_sems, out_sems
        self.chunks = []
        per_pool = [0, 0]
        for j, dst in enumerate(dsts):
            pool = 0 if dst.shape[1] > D_MODEL else 1
            for r0 in range(0, dst.shape[0], stages[pool].shape[1]):
                self.chunks.append((pool, per_pool[pool], j, r0))
                per_pool[pool] += 1
        self.pos = 0
        self.started = 0
        self.done = [0, 0]
        self.writebacks = []

    def _copy(self, pool, k, j, r0):
        rows, cols = self.stages[pool].shape[1], self.dsts[j].shape[1]
        slot = k % CONVERT_SLOTS
        return pltpu.make_async_copy(self.srcs[j].at[pl.ds(r0, rows), :],
                                     self.stages[pool].at[slot, :, pl.ds(0, cols)],
                                     self.in_sems[pool].at[slot])

    def ready(self, j_last):
        chunks = self.chunks
        while self.pos < len(chunks) and chunks[self.pos][2] <= j_last:
            while (self.started < len(chunks)
                   and chunks[self.started][1] < self.done[chunks[self.started][0]] + CONVERT_SLOTS):
                self._copy(*chunks[self.started]).start()
                self.started += 1
            pool, k, j, r0 = chunks[self.pos]
            self._copy(pool, k, j, r0).wait()
            rows, cols = self.stages[pool].shape[1], self.dsts[j].shape[1]
            self.dsts[j][r0:r0 + rows, :] = self.stages[pool][k % CONVERT_SLOTS, :, :cols].astype(BF16)
            self.done[pool] += 1
            self.pos += 1
            if self.pos == len(chunks) or chunks[self.pos][2] != j:
                self.writebacks.append(
                    pltpu.make_async_copy(self.dsts[j], self.outs[j], self.out_sems.at[j]))
                self.writebacks[-1].start()

    def finish(self):
        self.ready(len(self.dsts) - 1)
        for writeback in self.writebacks:
            writeback.wait()


def _layer_kernel(*refs, cfg):
    it = iter(refs)
    x_ref = next(it)
    if cfg.conv:
        if cfg.streams:
            p1_ref, p2_ref = next(it), next(it)
        w_in_ref, cw_ref, w_mix_ref = next(it), next(it), next(it)
    else:
        a_ref, w_mix_ref = next(it), next(it)
    g1_ref, b1_ref, g2_ref, b2_ref = next(it), next(it), next(it), next(it)
    w_up_ref, w_down_ref = next(it), next(it)
    if cfg.q_scale:
        w_q_ref = next(it)
    y_ref = next(it)
    if cfg.q_scale:
        q_ref = next(it)
    if cfg.conv:
        ztail_ref = next(it)
    if cfg.convert:
        n_w = 4 if cfg.conv else 3
        outs = [next(it) for _ in range(n_w)]
    if cfg.conv and not cfg.streams:
        carry_ref = next(it)
    if cfg.convert:
        bufs = [next(it) for _ in range(n_w)]
        stages, in_sems, out_sems = (next(it), next(it)), (next(it), next(it)), next(it)
        mix_l, mlp_l = cfg.convert
        srcs = ([w_in_ref.at[mix_l]] if cfg.conv else []) + [
            w_mix_ref.at[mix_l], w_up_ref.at[mlp_l], w_down_ref.at[mlp_l]]
        converter = _WeightConverter(srcs, bufs, outs, stages, in_sems, out_sems)
        if cfg.conv:
            w_in_ref = bufs[0]
        w_mix_ref, w_up_ref, w_down_ref = bufs[-3:]

    def weight_ready(ref):
        if cfg.convert:
            converter.ready(next(j for j, buf in enumerate(bufs) if buf is ref))

    if cfg.conv and not cfg.streams:
        @pl.when(pl.program_id(0) == 0)
        def _():
            carry_ref[...] = jnp.zeros_like(carry_ref)
        prev = carry_ref[...]

    sm = cfg.tm // cfg.n_sub
    subs = range(cfg.n_sub)
    rows = [slice(sub * sm, (sub + 1) * sm) for sub in subs]
    x = [x_ref[r, :] for r in rows]
    if cfg.conv:
        weight_ready(w_in_ref)
        bch = [_dot(x[i].astype(BF16), w_in_ref[...]) for i in subs]
        row = lax.broadcasted_iota(jnp.int32, (sm, D_MODEL), 0)
        cw = cw_ref[...]
        mix_in = []
        for i in subs:
            b_gate = bch[i][:, :D_MODEL]
            z = bch[i][:, D_MODEL:2 * D_MODEL] * bch[i][:, 2 * D_MODEL:]
            r1 = pltpu.roll(z, 1, 0)
            r2 = pltpu.roll(z, 2, 0)
            if cfg.streams:
                rm = row & (cfg.stream_len - 1)
                zm1 = jnp.where(rm == 0, p1_ref[rows[i], :], r1)
                zm2 = jnp.where(rm < 2, p2_ref[rows[i], :], r2)
                ztail_ref[rows[i], :] = z
            else:
                pm1 = prev[SUBLANES_V7X - 1:SUBLANES_V7X, :]
                pm2 = prev[SUBLANES_V7X - 2:SUBLANES_V7X - 1, :]
                zm1 = jnp.where(row == 0, pm1, r1)
                zm2 = jnp.where(row == 0, pm2, jnp.where(row == 1, pm1, r2))
                prev = z[sm - SUBLANES_V7X:, :]
            yc = cw[0:1, :] * zm2 + cw[1:2, :] * zm1 + cw[2:3, :] * z
            mix_in.append((b_gate * yc).astype(BF16))
        if not cfg.streams:
            carry_ref[...] = prev
            ztail_ref[...] = prev
    elif cfg.pairs:
        mix_in = [jnp.concatenate([a_ref[p, r, :] for p in range(N_PAIRS)], axis=-1) for r in rows]
    else:
        mix_in = [a_ref[r, :] for r in rows]
    weight_ready(w_mix_ref)
    h = [_dot(mix_in[i], w_mix_ref[...]) for i in subs]
    x1 = [_layer_norm(ALPHA * x[i] + h[i], g1_ref[...], b1_ref[...]) for i in subs]
    weight_ready(w_up_ref)
    u = [_dot(x1[i].astype(BF16), w_up_ref[...]) for i in subs]
    u = [jnp.maximum(u[i], 0.0) for i in subs]
    weight_ready(w_down_ref)
    m = [_dot((u[i] * u[i]).astype(BF16), w_down_ref[...]) for i in subs]
    y = [_layer_norm(ALPHA * x1[i] + m[i], g2_ref[...], b2_ref[...]) for i in subs]
    for i in subs:
        y_ref[rows[i], :] = y[i]
    if cfg.q_scale:
        q = [_dot(y[i].astype(BF16), w_q_ref[...]) * cfg.q_scale for i in subs]
        for i in subs:
            for p in range(N_PAIRS):
                q_ref[p, rows[i], :] = q[i][:, p * PAIR_W:(p + 1) * PAIR_W].astype(BF16)
    if cfg.convert:
        converter.finish()


def _layer_call(x, mix_args, ln, w_up, w_down, *, conv, stream_len, pairs, tm, n_sub,
                w_q=None, q_scale=0.0, convert=False):
    s = x.shape[0]
    assert s % tm == 0 and tm % n_sub == 0
    assert stream_len & (stream_len - 1) == 0 and (stream_len == 0 or (tm // n_sub) % stream_len == 0)
    assert (w_q is None) == (q_scale == 0.0)
    n = s // tm
    streams = stream_len > 0
    row_spec = pl.BlockSpec((tm, D_MODEL), lambda i: (i, 0))
    pair_spec = pl.BlockSpec((N_PAIRS, tm, PAIR_W), lambda i: (0, i, 0))
    hbm_spec = pl.BlockSpec(memory_space=pl.ANY)
    w_spec = (lambda w: hbm_spec) if convert else (lambda w: w.spec())
    in_specs = [row_spec]
    args = [x]
    if conv:
        if streams:
            p1, p2, w_in, cw, w_mix = mix_args
            in_specs += [row_spec, row_spec]
            args += [p1, p2]
        else:
            w_in, cw, w_mix = mix_args
        in_specs += [w_spec(w_in), cw.spec(), w_spec(w_mix)]
        args += [w_in.array, cw.array, w_mix.array]
        converted = [w_in, w_mix, w_up, w_down]
    else:
        a, w_mix = mix_args
        in_specs += [pair_spec if pairs else row_spec, w_spec(w_mix)]
        args += [a, w_mix.array]
        converted = [w_mix, w_up, w_down]
    in_specs += [_const_spec((1, D_MODEL))] * 4 + [w_spec(w_up), w_spec(w_down)]
    args += list(ln) + [w_up.array, w_down.array]

    out_shape = [jax.ShapeDtypeStruct((s, D_MODEL), F32)]
    out_specs = [row_spec]
    if w_q is not None:
        in_specs.append(w_q.spec())
        args.append(w_q.array)
        out_shape.append(jax.ShapeDtypeStruct((N_PAIRS, s, PAIR_W), BF16))
        out_specs.append(pair_spec)
    scratch = []
    if conv:
        if streams:
            out_shape.append(jax.ShapeDtypeStruct((s, D_MODEL), F32))
            out_specs.append(row_spec)
        else:
            out_shape.append(jax.ShapeDtypeStruct((SUBLANES_V7X, D_MODEL), F32))
            out_specs.append(pl.BlockSpec((SUBLANES_V7X, D_MODEL), lambda i: (0, 0)))
            scratch.append(pltpu.VMEM((SUBLANES_V7X, D_MODEL), F32))
    if convert:
        assert n == 1 and all(w.layer is not None for w in converted)
        out_shape += [jax.ShapeDtypeStruct(w.shape, BF16) for w in converted]
        out_specs += [hbm_spec] * len(converted)
        scratch += [pltpu.VMEM(w.shape, BF16) for w in converted]
        wide = max(w.shape[1] for w in converted)
        chunk_rows = (CONVERT_CHUNK_BYTES // (4 * wide), CONVERT_CHUNK_BYTES // (4 * D_MODEL))
        assert all(w.shape[0] % chunk_rows[0 if w.shape[1] > D_MODEL else 1] == 0 for w in converted)
        scratch += [pltpu.VMEM((CONVERT_SLOTS, chunk_rows[0], wide), F32),
                    pltpu.VMEM((CONVERT_SLOTS, chunk_rows[1], D_MODEL), F32),
                    pltpu.SemaphoreType.DMA((CONVERT_SLOTS,)), pltpu.SemaphoreType.DMA((CONVERT_SLOTS,)),
                    pltpu.SemaphoreType.DMA((len(converted),))]
    cfg = _LayerCfg(conv=conv, stream_len=stream_len, pairs=pairs, tm=tm, n_sub=n_sub, q_scale=q_scale,
                    convert=(w_mix.layer, w_up.layer) if convert else None)
    return pl.pallas_call(
        functools.partial(_layer_kernel, cfg=cfg),
        grid=(n,),
        in_specs=in_specs,
        out_specs=out_specs,
        out_shape=out_shape,
        scratch_shapes=scratch,
        compiler_params=_params(),
        name="layer_conv" if conv else "layer_attn",
    )(*args)


def _proj_kernel(*refs, scales, tails, tail_rows, pairs):
    n = len(scales)
    x_ref = refs[0]
    w_refs = refs[1:1 + n]
    out_refs = list(refs[1 + n:])
    xb = x_ref[...].astype(BF16)
    tm = xb.shape[0]
    for j in range(n):
        pj = _dot(xb, w_refs[j][...])
        if scales[j] != 1.0:
            pj = pj * scales[j]
        if pairs:
            _store_pairs(out_refs.pop(0), pj)
        else:
            out_refs.pop(0)[...] = pj.astype(BF16)
        if tails[j]:
            out_refs.pop(0)[...] = pj[tm - tail_rows:, :]


def _proj_call(x, ws, scales, tails, *, tm, keep, pairs):
    s = x.shape[0]
    assert s % tm == 0
    n = s // tm
    tail_rows = min(tm, keep)
    assert keep % tail_rows == 0
    n_tail_blocks = keep // tail_rows
    row_spec = pl.BlockSpec((tm, D_MODEL), lambda i: (i, 0))
    pair_spec = pl.BlockSpec((N_PAIRS, tm, PAIR_W), lambda i: (0, i, 0))
    tail_spec = pl.BlockSpec((tail_rows, D_MODEL),
                             lambda i: (jnp.maximum(i - (n - n_tail_blocks), 0), 0))
    out_shape, out_specs = [], []
    for t in tails:
        if pairs:
            out_shape.append(jax.ShapeDtypeStruct((N_PAIRS, s, PAIR_W), BF16))
            out_specs.append(pair_spec)
        else:
            out_shape.append(jax.ShapeDtypeStruct((s, D_MODEL), BF16))
            out_specs.append(row_spec)
        if t:
            out_shape.append(jax.ShapeDtypeStruct((keep, D_MODEL), F32))
            out_specs.append(tail_spec)
    return pl.pallas_call(
        functools.partial(_proj_kernel, scales=tuple(scales), tails=tuple(tails),
                          tail_rows=tail_rows, pairs=pairs),
        grid=(n,),
        in_specs=[row_spec] + [w.spec() for w in ws],
        out_specs=out_specs,
        out_shape=out_shape,
        compiler_params=_params(),
        name="proj",
    )(x, *[w.array for w in ws])


def _rel_bias_rows(rvec_ref, h, rows, valid):
    row = jnp.broadcast_to(rvec_ref[h:h + 1, :], (rows, REL_RING))
    t = pltpu.roll(row, 0, 1, stride=1, stride_axis=0)
    return jnp.where(valid, t * LOG2E, MASK_VALUE)


def _softmax_weights(s):
    m = jnp.max(s, axis=-1, keepdims=True)
    e = jnp.exp2(s - m)
    return e.astype(BF16), jnp.sum(e, axis=-1, keepdims=True)


def _weighted_values(eb, l, v_parts):
    o = None
    off = 0
    for v in v_parts:
        part = _dot(eb[:, off:off + v.shape[0]], v)
        o = part if o is None else o + part
        off += v.shape[0]
    return o / l


def _attn_prompt_kernel(q_ref, kp_ref, kc_ref, vp_ref, vc_ref, rvec_ref, o_ref, bias_ref):
    step = pl.program_id(0)
    low = lax.broadcasted_iota(jnp.int32, (Q_BLOCK, PAIR_W), 1) < HEAD_DIM
    n_blocks = ATT_STEP // Q_BLOCK
    missing = [max(BAND_PAST - b * Q_BLOCK, 0) for b in range(n_blocks)]

    def build_bias(first_key):
        qi = lax.broadcasted_iota(jnp.int32, (Q_BLOCK, REL_RING), 0)
        kj = lax.broadcasted_iota(jnp.int32, (Q_BLOCK, REL_RING), 1)
        start = (qi >> 6) << 6
        valid = (kj >= start) & (kj < start + (BAND_PAST + CHUNK)) & (kj >= first_key)
        for h in range(N_HEADS):
            bias_ref[h * Q_BLOCK:(h + 1) * Q_BLOCK, :] = (
                _rel_bias_rows(rvec_ref, h, Q_BLOCK, valid)[:, :K_BLOCK])

    for blk in range(n_blocks):
        if blk == 0:
            rebuild = (step == 0) | (step == 1) if missing[-1] else (step == 0)
        else:
            rebuild = (step == 0) if missing[blk] != missing[blk - 1] else None
        if rebuild is not None:
            pl.when(rebuild)(functools.partial(
                build_bias, jnp.where(step == 0, missing[blk], 0)))

        q_rows = slice(blk * Q_BLOCK, (blk + 1) * Q_BLOCK)
        past_lo = blk * Q_BLOCK
        cur_rows = slice(max(past_lo - BAND_PAST, 0), past_lo + Q_BLOCK)

        def window(prev_ref, cur_ref, p):
            parts = [prev_ref[p, past_lo:, :]] if past_lo < BAND_PAST else []
            return parts + [cur_ref[p, cur_rows, :]]

        def scores(p):
            qb = q_ref[p, q_rows, :]
            qs = jnp.concatenate([jnp.where(low, qb, jnp.zeros_like(qb)),
                                  jnp.where(low, jnp.zeros_like(qb), qb)], axis=0)
            s = jnp.concatenate([_dot_nt(qs, k) for k in window(kp_ref, kc_ref, p)], axis=-1)
            return s + bias_ref[2 * p * Q_BLOCK:2 * (p + 1) * Q_BLOCK, :]

        def finish(p, s):
            o = _weighted_values(*_softmax_weights(s), window(vp_ref, vc_ref, p))
            o_ref[p, q_rows, :] = jnp.where(low, o[:Q_BLOCK], o[Q_BLOCK:]).astype(BF16)

        s_next = scores(0)
        for p in range(N_PAIRS):
            s_cur = s_next
            if p + 1 < N_PAIRS:
                s_next = scores(p + 1)
            finish(p, s_cur)


def _attn_prompt_call(q, k, v, rvec):
    s = q.shape[1]
    assert s % ATT_STEP == 0 and ATT_STEP % BAND_PAST == 0
    n = s // ATT_STEP
    ratio = ATT_STEP // BAND_PAST
    cur = pl.BlockSpec((N_PAIRS, ATT_STEP, PAIR_W), lambda i: (0, i, 0))
    prev = pl.BlockSpec((N_PAIRS, BAND_PAST, PAIR_W), lambda i: (0, jnp.maximum(ratio * i - 1, 0), 0))
    return pl.pallas_call(
        _attn_prompt_kernel,
        grid=(n,),
        in_specs=[cur, prev, cur, prev, cur, _const_spec(rvec.shape)],
        out_specs=cur,
        out_shape=jax.ShapeDtypeStruct((N_PAIRS, s, PAIR_W), BF16),
        scratch_shapes=[pltpu.VMEM((N_HEADS * Q_BLOCK, K_BLOCK), F32)],
        compiler_params=_params(),
        name="attn_prompt",
    )(q, k, k, v, v, rvec)


def _attn_decode_kernel(q_ref, kn_ref, vn_ref, kc_ref, vc_ref, rvec_ref, o_ref, bias_ref):
    t = q_ref.shape[0]
    w = kc_ref.shape[1]
    kpad = bias_ref.shape[1]

    @pl.when(pl.program_id(0) == 0)
    def _():
        kj = lax.broadcasted_iota(jnp.int32, (t, REL_RING), 1)
        valid = kj < w + t
        for h in range(N_HEADS):
            bias_ref[h * t:(h + 1) * t, :] = _rel_bias_rows(rvec_ref, h, t, valid)[:, :kpad]

    q = q_ref[...]
    head_of_lane = lax.broadcasted_iota(jnp.int32, (t, D_MODEL), 1) >> 6
    qs = jnp.concatenate(
        [jnp.where(head_of_lane == h, q, jnp.zeros_like(q)) for h in range(N_HEADS)], axis=0)
    zeros = jnp.zeros((kpad - w - t, D_MODEL), BF16)
    kk = jnp.concatenate([kc_ref[0].astype(BF16), kn_ref[...], zeros], axis=0)
    vv = jnp.concatenate([vc_ref[0].astype(BF16), vn_ref[...], zeros], axis=0)
    eb, l = _softmax_weights(_dot_nt(qs, kk) + bias_ref[...])
    o = _weighted_values(eb, l, (vv,))
    out = jnp.zeros((t, D_MODEL), F32)
    for h in range(N_HEADS):
        out = out + jnp.where(head_of_lane == h, o[h * t:(h + 1) * t, :], 0.0)
    o_ref[...] = out.astype(BF16)


def _attn_decode_call(q, kn, vn, cache_k, cache_v, rvec, *, t):
    nb, w = cache_k.shape[0], cache_k.shape[1]
    assert w == BAND_PAST
    kpad = -(-(w + t) // LANES_V7X) * LANES_V7X
    assert kpad + t <= REL_RING
    new_spec = pl.BlockSpec((t, D_MODEL), lambda i: (i, 0))
    cache_spec = pl.BlockSpec((1, w, D_MODEL), lambda i: (i, 0, 0))
    return pl.pallas_call(
        _attn_decode_kernel,
        grid=(nb,),
        in_specs=[new_spec, new_spec, new_spec, cache_spec, cache_spec, _const_spec(rvec.shape)],
        out_specs=new_spec,
        out_shape=jax.ShapeDtypeStruct((nb * t, D_MODEL), BF16),
        scratch_shapes=[pltpu.VMEM((N_HEADS * t, kpad), F32)],
        compiler_params=_params(),
        name="attn_decode",
    )(q, kn, vn, cache_k, cache_v, rvec)


def _rel_vector(table):
    m = jnp.arange(REL_RING)
    d = jnp.where(m < K_BLOCK, m, m - REL_RING)
    idx = jnp.clip(BAND_PAST - d, -REL_MAX, REL_MAX) + REL_MAX
    return table[idx].T.astype(F32)


def _trunk(x, conv_prev, caches, wts, layer_bf16, *, tm, n_sub, proj_tm, keep):
    s = x.shape[0]
    ln = lambda l: (wts["ln_mix_g"][l][None], wts["ln_mix_b"][l][None],
                    wts["ln_ffn_g"][l][None], wts["ln_ffn_b"][l][None])
    stream_len = 0 if conv_prev is None else s // conv_prev.shape[1]
    pairs = caches is None
    convert = layer_bf16 is None
    converted = []
    ztails = []
    for l in range(N_A):
        if convert:
            w_in, w_out = _Weight(wts["w_in_a"], l), _Weight(wts["w_out_a"], l)
            w_up, w_down = _Weight(wts["w_up"], l), _Weight(wts["w_down"], l)
        else:
            w_in, w_out, w_up, w_down = (_Weight(w) for w in layer_bf16[l])
        mix = (w_in, _Weight(wts["conv_w_a"], l), w_out)
        if stream_len:
            p2 = jnp.pad(conv_prev[l], ((0, 0), (0, stream_len - 2), (0, 0))).reshape(s, D_MODEL)
            p1 = jnp.pad(conv_prev[l][:, 1:], ((0, 0), (0, stream_len - 1), (0, 0))).reshape(s, D_MODEL)
            mix = (p1, p2) + mix
        x, zt, *bf16 = _layer_call(x, mix, ln(l), w_up, w_down, conv=True, stream_len=stream_len,
                                   pairs=False, tm=tm, n_sub=n_sub, convert=convert)
        ztails.append(zt)
        converted.append(bf16)

    q_scale = HEAD_DIM ** -0.5 * LOG2E
    q, k, kf, v, vf = _proj_call(
        x, (_Weight(wts["w_q_bf16"], 0), _Weight(wts["w_k_bf16"]), _Weight(wts["w_v_bf16"])),
        (q_scale, 1.0, 1.0), (False, True, True), tm=proj_tm, keep=keep, pairs=pairs)
    n_b = DEPTH - N_A
    for j in range(n_b):
        l = N_A + j
        rvec = _rel_vector(wts["rel_bias_b"][j])
        if caches is None:
            a = _attn_prompt_call(q, k, v, rvec)
        else:
            a = _attn_decode_call(q, k, v, caches[0], caches[1], rvec, t=s // caches[0].shape[0])
        if convert:
            w_o = _Weight(wts["w_o_b"], j)
            w_up, w_down = _Weight(wts["w_up"], l), _Weight(wts["w_down"], l)
        else:
            w_o, w_up, w_down = (_Weight(w) for w in layer_bf16[l])
        fuse_q = pairs and j + 1 < n_b
        outs = _layer_call(x, (a, w_o), ln(l), w_up, w_down,
                           conv=False, stream_len=0, pairs=pairs, tm=tm, n_sub=n_sub,
                           w_q=_Weight(wts["w_q_bf16"], j + 1) if fuse_q else None,
                           q_scale=q_scale if fuse_q else 0.0, convert=convert)
        x = outs[0]
        if fuse_q:
            q = outs[1]
        elif j + 1 < n_b:
            (q,) = _proj_call(x, (_Weight(wts["w_q_bf16"], j + 1),), (q_scale,), (False,),
                              tm=proj_tm, keep=keep, pairs=pairs)
        converted.append(outs[len(outs) - 3:] if convert else [])
    return x, ztails, kf, vf, converted


def kernel(x_prompt, x_sample, cache_conv, cache_k, cache_v, ln_mix_g, ln_mix_b, ln_ffn_g, ln_ffn_b,
           w_up, w_down, w_in_a, conv_w_a, w_out_a, w_k, w_v, w_q_b, w_o_b, rel_bias_b):
    assert x_prompt.shape[0] == 1
    wts = dict(
        ln_mix_g=ln_mix_g, ln_mix_b=ln_mix_b, ln_ffn_g=ln_ffn_g, ln_ffn_b=ln_ffn_b,
        w_up=w_up, w_down=w_down, w_in_a=w_in_a, conv_w_a=conv_w_a, w_out_a=w_out_a, w_o_b=w_o_b,
        w_k_bf16=w_k.astype(BF16), w_v_bf16=w_v.astype(BF16), w_q_bf16=w_q_b.astype(BF16),
        rel_bias_b=rel_bias_b)

    nb, t = x_sample.shape[0], x_sample.shape[1]
    w = cache_k.shape[1]
    ns = nb * t
    ys, zts, kfs, vfs, layer_bf16 = _trunk(
        x_sample.reshape(ns, D_MODEL), cache_conv,
        (cache_k.reshape(nb, w, D_MODEL), cache_v.reshape(nb, w, D_MODEL)),
        wts, None, tm=ns, n_sub=1, proj_tm=ns, keep=ns)
    conv_sample = jnp.stack([zt.reshape(nb, t, D_MODEL)[:, t - 2:] for zt in zts])
    k_sample = kfs.reshape(nb, t, N_HEADS, HEAD_DIM)
    v_sample = vfs.reshape(nb, t, N_HEADS, HEAD_DIM)

    seq = x_prompt.shape[1]
    keep = min(BAND_PAST, seq)
    yp, ztp, kfp, vfp, _ = _trunk(x_prompt[0], None, None, wts, layer_bf16,
                                  tm=LAYER_TILE_ROWS, n_sub=LAYER_SUB_TILES, proj_tm=PROJ_TILE_ROWS,
                                  keep=keep)
    conv_prompt = jnp.stack([zt[SUBLANES_V7X - 2:] for zt in ztp])[:, None]
    k_prompt = kfp.reshape(1, keep, N_HEADS, HEAD_DIM)
    v_prompt = vfp.reshape(1, keep, N_HEADS, HEAD_DIM)
    return (yp[None], ys.reshape(nb, t, D_MODEL), conv_prompt, k_prompt, v_prompt,
            conv_sample, k_sample, v_sample)
```

```python
import dataclasses
import functools
import math

import jax
import jax.numpy as jnp
from jax import lax
from jax.experimental import pallas as pl
from jax.experimental.pallas import tpu as pltpu

D_MODEL = 1024
N_HEADS = 16
HEAD_DIM = 64
DEPTH = 4
N_A = 2
CHUNK = 64
N_PREV_CHUNKS = 8
BAND_PAST = N_PREV_CHUNKS * CHUNK
REL_MAX = 128
ALPHA = (2.0 * DEPTH) ** 0.25
LN_EPS = 1e-5
LOG2E = math.log2(math.e)

LANES_V7X = 128
SUBLANES_V7X = 8
VMEM_BYTES_V7X = 64 * 1024 * 1024
VMEM_LIMIT_BYTES = VMEM_BYTES_V7X - 8 * 1024 * 1024

PAIR_W = LANES_V7X
N_PAIRS = D_MODEL // PAIR_W
Q_BLOCK = 2 * CHUNK
K_BLOCK = BAND_PAST + Q_BLOCK
ATT_STEP = BAND_PAST
LAYER_TILE_ROWS = 512
LAYER_SUB_TILES = 2
PROJ_TILE_ROWS = 1024
CONVERT_CHUNK_BYTES = 2 * 1024 * 1024
CONVERT_SLOTS = 4
REL_RING = 1024
MASK_VALUE = -1e30

BF16 = jnp.bfloat16
F32 = jnp.float32


def _const_spec(shape):
    zeros = (0,) * len(shape)
    return pl.BlockSpec(shape, lambda i: zeros, pipeline_mode=pl.Buffered(1))


@dataclasses.dataclass(frozen=True)
class _Weight:
    array: jax.Array
    layer: int | None = None

    @property
    def shape(self):
        return self.array.shape if self.layer is None else self.array.shape[1:]

    def spec(self):
        if self.layer is None:
            return _const_spec(self.array.shape)
        index = (self.layer,) + (0,) * len(self.shape)
        return pl.BlockSpec((None,) + self.shape, lambda i: index, pipeline_mode=pl.Buffered(1))


def _params():
    return pltpu.CompilerParams(dimension_semantics=("arbitrary",), vmem_limit_bytes=VMEM_LIMIT_BYTES)


def _dot(a, b):
    return jnp.dot(a, b, preferred_element_type=F32)


def _dot_nt(a, b):
    return lax.dot_general(a, b, (((1,), (1,)), ((), ())), preferred_element_type=F32)


def _layer_norm(x, g, b):
    mu = jnp.mean(x, axis=-1, keepdims=True)
    xc = x - mu
    var = jnp.mean(xc * xc, axis=-1, keepdims=True)
    return xc * lax.rsqrt(var + LN_EPS) * g + b


def _store_pairs(ref, val):
    for p in range(N_PAIRS):
        ref[p] = val[:, p * PAIR_W:(p + 1) * PAIR_W].astype(BF16)


@dataclasses.dataclass(frozen=True)
class _LayerCfg:
    conv: bool
    stream_len: int
    pairs: bool
    tm: int
    n_sub: int
    q_scale: float
    convert: tuple[int, int] | None = None

    @property
    def streams(self):
        return self.stream_len > 0


class _WeightConverter:
    def __init__(self, srcs, dsts, outs, stages, in_sems, out_sems):
        self.srcs, self.dsts, self.outs = srcs, dsts, outs
        self.stages, self.in_sems, self.out_sems = stages, in_sems, out_sems
        self.chunks = []
        per_pool = [0, 0]
        for j, dst in enumerate(dsts):
            pool = 0 if dst.shape[1] > D_MODEL else 1
            for r0 in range(0, dst.shape[0], stages[pool].shape[1]):
                self.chunks.append((pool, per_pool[pool], j, r0))
                per_pool[pool] += 1
        self.pos = 0
        self.started = 0
        self.done = [0, 0]
        self.writebacks = []

    def _copy(self, pool, k, j, r0):
        rows, cols = self.stages[pool].shape[1], self.dsts[j].shape[1]
        slot = k % CONVERT_SLOTS
        return pltpu.make_async_copy(self.srcs[j].at[pl.ds(r0, rows), :],
                                     self.stages[pool].at[slot, :, pl.ds(0, cols)],
                                     self.in_sems[pool].at[slot])

    def ready(self, j_last):
        chunks = self.chunks
        while self.pos < len(chunks) and chunks[self.pos][2] <= j_last:
            while (self.started < len(chunks)
                   and chunks[self.started][1] < self.done[chunks[self.started][0]] + CONVERT_SLOTS):
                self._copy(*chunks[self.started]).start()
                self.started += 1
            pool, k, j, r0 = chunks[self.pos]
            self._copy(pool, k, j, r0).wait()
            rows, cols = self.stages[pool].shape[1], self.dsts[j].shape[1]
            self.dsts[j][r0:r0 + rows, :] = self.stages[pool][k % CONVERT_SLOTS, :, :cols].astype(BF16)
            self.done[pool] += 1
            self.pos += 1
        if self.pos == len(chunks) and not self.writebacks:
            for j, dst in enumerate(self.dsts):
                self.writebacks.append(pltpu.make_async_copy(dst, self.outs[j], self.out_sems.at[j]))
                self.writebacks[-1].start()

    def finish(self):
        self.ready(len(self.dsts) - 1)
        for writeback in self.writebacks:
            writeback.wait()


def _layer_kernel(*refs, cfg):
    it = iter(refs)
    x_ref = next(it)
    if cfg.conv:
        if cfg.streams:
            p1_ref, p2_ref = next(it), next(it)
        w_in_ref, cw_ref, w_mix_ref = next(it), next(it), next(it)
    else:
        a_ref, w_mix_ref = next(it), next(it)
    g1_ref, b1_ref, g2_ref, b2_ref = next(it), next(it), next(it), next(it)
    w_up_ref, w_down_ref = next(it), next(it)
    if cfg.q_scale:
        w_q_ref = next(it)
    y_ref = next(it)
    if cfg.q_scale:
        q_ref = next(it)
    if cfg.conv:
        ztail_ref = next(it)
    if cfg.convert:
        n_w = 4 if cfg.conv else 3
        outs = [next(it) for _ in range(n_w)]
    if cfg.conv and not cfg.streams:
        carry_ref = next(it)
    if cfg.convert:
        bufs = [next(it) for _ in range(n_w)]
        stages, in_sems, out_sems = (next(it), next(it)), (next(it), next(it)), next(it)
        mix_l, mlp_l = cfg.convert
        srcs = ([w_in_ref.at[mix_l]] if cfg.conv else []) + [
            w_mix_ref.at[mix_l], w_up_ref.at[mlp_l], w_down_ref.at[mlp_l]]
        converter = _WeightConverter(srcs, bufs, outs, stages, in_sems, out_sems)
        if cfg.conv:
            w_in_ref = bufs[0]
        w_mix_ref, w_up_ref, w_down_ref = bufs[-3:]

    def weight_ready(ref):
        if cfg.convert:
            converter.ready(next(j for j, buf in enumerate(bufs) if buf is ref))

    if cfg.conv and not cfg.streams:
        @pl.when(pl.program_id(0) == 0)
        def _():
            carry_ref[...] = jnp.zeros_like(carry_ref)
        prev = carry_ref[...]

    sm = cfg.tm // cfg.n_sub
    subs = range(cfg.n_sub)
    rows = [slice(sub * sm, (sub + 1) * sm) for sub in subs]
    x = [x_ref[r, :] for r in rows]
    if cfg.conv:
        weight_ready(w_in_ref)
        bch = [_dot(x[i].astype(BF16), w_in_ref[...]) for i in subs]
        row = lax.broadcasted_iota(jnp.int32, (sm, D_MODEL), 0)
        cw = cw_ref[...]
        mix_in = []
        for i in subs:
            b_gate = bch[i][:, :D_MODEL]
            z = bch[i][:, D_MODEL:2 * D_MODEL] * bch[i][:, 2 * D_MODEL:]
            r1 = pltpu.roll(z, 1, 0)
            r2 = pltpu.roll(z, 2, 0)
            if cfg.streams:
                rm = row & (cfg.stream_len - 1)
                zm1 = jnp.where(rm == 0, p1_ref[rows[i], :], r1)
                zm2 = jnp.where(rm < 2, p2_ref[rows[i], :], r2)
                ztail_ref[rows[i], :] = z
            else:
                pm1 = prev[SUBLANES_V7X - 1:SUBLANES_V7X, :]
                pm2 = prev[SUBLANES_V7X - 2:SUBLANES_V7X - 1, :]
                zm1 = jnp.where(row == 0, pm1, r1)
                zm2 = jnp.where(row == 0, pm2, jnp.where(row == 1, pm1, r2))
                prev = z[sm - SUBLANES_V7X:, :]
            yc = cw[0:1, :] * zm2 + cw[1:2, :] * zm1 + cw[2:3, :] * z
            mix_in.append((b_gate * yc).astype(BF16))
        if not cfg.streams:
            carry_ref[...] = prev
            ztail_ref[...] = prev
    elif cfg.pairs:
        mix_in = [jnp.concatenate([a_ref[p, r, :] for p in range(N_PAIRS)], axis=-1) for r in rows]
    else:
        mix_in = [a_ref[r, :] for r in rows]
    weight_ready(w_mix_ref)
    h = [_dot(mix_in[i], w_mix_ref[...]) for i in subs]
    x1 = [_layer_norm(ALPHA * x[i] + h[i], g1_ref[...], b1_ref[...]) for i in subs]
    weight_ready(w_up_ref)
    u = [_dot(x1[i].astype(BF16), w_up_ref[...]) for i in subs]
    u = [jnp.maximum(u[i], 0.0) for i in subs]
    weight_ready(w_down_ref)
    m = [_dot((u[i] * u[i]).astype(BF16), w_down_ref[...]) for i in subs]
    y = [_layer_norm(ALPHA * x1[i] + m[i], g2_ref[...], b2_ref[...]) for i in subs]
    for i in subs:
        y_ref[rows[i], :] = y[i]
    if cfg.q_scale:
        q = [_dot(y[i].astype(BF16), w_q_ref[...]) * cfg.q_scale for i in subs]
        for i in subs:
            for p in range(N_PAIRS):
                q_ref[p, rows[i], :] = q[i][:, p * PAIR_W:(p + 1) * PAIR_W].astype(BF16)
    if cfg.convert:
        converter.finish()


def _layer_call(x, mix_args, ln, w_up, w_down, *, conv, stream_len, pairs, tm, n_sub,
                w_q=None, q_scale=0.0, convert=False):
    s = x.shape[0]
    assert s % tm == 0 and tm % n_sub == 0
    assert stream_len & (stream_len - 1) == 0 and (stream_len == 0 or (tm // n_sub) % stream_len == 0)
    assert (w_q is None) == (q_scale == 0.0)
    n = s // tm
    streams = stream_len > 0
    row_spec = pl.BlockSpec((tm, D_MODEL), lambda i: (i, 0))
    pair_spec = pl.BlockSpec((N_PAIRS, tm, PAIR_W), lambda i: (0, i, 0))
    hbm_spec = pl.BlockSpec(memory_space=pl.ANY)
    w_spec = (lambda w: hbm_spec) if convert else (lambda w: w.spec())
    in_specs = [row_spec]
    args = [x]
    if conv:
        if streams:
            p1, p2, w_in, cw, w_mix = mix_args
            in_specs += [row_spec, row_spec]
            args += [p1, p2]
        else:
            w_in, cw, w_mix = mix_args
        in_specs += [w_spec(w_in), cw.spec(), w_spec(w_mix)]
        args += [w_in.array, cw.array, w_mix.array]
        converted = [w_in, w_mix, w_up, w_down]
    else:
        a, w_mix = mix_args
        in_specs += [pair_spec if pairs else row_spec, w_spec(w_mix)]
        args += [a, w_mix.array]
        converted = [w_mix, w_up, w_down]
    in_specs += [_const_spec((1, D_MODEL))] * 4 + [w_spec(w_up), w_spec(w_down)]
    args += list(ln) + [w_up.array, w_down.array]

    out_shape = [jax.ShapeDtypeStruct((s, D_MODEL), F32)]
    out_specs = [row_spec]
    if w_q is not None:
        in_specs.append(w_q.spec())
        args.append(w_q.array)
        out_shape.append(jax.ShapeDtypeStruct((N_PAIRS, s, PAIR_W), BF16))
        out_specs.append(pair_spec)
    scratch = []
    if conv:
        if streams:
            out_shape.append(jax.ShapeDtypeStruct((s, D_MODEL), F32))
            out_specs.append(row_spec)
        else:
            out_shape.append(jax.ShapeDtypeStruct((SUBLANES_V7X, D_MODEL), F32))
            out_specs.append(pl.BlockSpec((SUBLANES_V7X, D_MODEL), lambda i: (0, 0)))
            scratch.append(pltpu.VMEM((SUBLANES_V7X, D_MODEL), F32))
    if convert:
        assert n == 1 and all(w.layer is not None for w in converted)
        out_shape += [jax.ShapeDtypeStruct(w.shape, BF16) for w in converted]
        out_specs += [hbm_spec] * len(converted)
        scratch += [pltpu.VMEM(w.shape, BF16) for w in converted]
        wide = max(w.shape[1] for w in converted)
        chunk_rows = (CONVERT_CHUNK_BYTES // (4 * wide), CONVERT_CHUNK_BYTES // (4 * D_MODEL))
        assert all(w.shape[0] % chunk_rows[0 if w.shape[1] > D_MODEL else 1] == 0 for w in converted)
        scratch += [pltpu.VMEM((CONVERT_SLOTS, chunk_rows[0], wide), F32),
                    pltpu.VMEM((CONVERT_SLOTS, chunk_rows[1], D_MODEL), F32),
                    pltpu.SemaphoreType.DMA((CONVERT_SLOTS,)), pltpu.SemaphoreType.DMA((CONVERT_SLOTS,)),
                    pltpu.SemaphoreType.DMA((len(converted),))]
    cfg = _LayerCfg(conv=conv, stream_len=stream_len, pairs=pairs, tm=tm, n_sub=n_sub, q_scale=q_scale,
                    convert=(w_mix.layer, w_up.layer) if convert else None)
    return pl.pallas_call(
        functools.partial(_layer_kernel, cfg=cfg),
        grid=(n,),
        in_specs=in_specs,
        out_specs=out_specs,
        out_shape=out_shape,
        scratch_shapes=scratch,
        compiler_params=_params(),
        name="layer_conv" if conv else "layer_attn",
    )(*args)


def _proj_kernel(*refs, scales, tails, tail_rows, pairs):
    n = len(scales)
    x_ref = refs[0]
    w_refs = refs[1:1 + n]
    out_refs = list(refs[1 + n:])
    xb = x_ref[...].astype(BF16)
    tm = xb.shape[0]
    for j in range(n):
        pj = _dot(xb, w_refs[j][...])
        if scales[j] != 1.0:
            pj = pj * scales[j]
        if pairs:
            _store_pairs(out_refs.pop(0), pj)
        else:
            out_refs.pop(0)[...] = pj.astype(BF16)
        if tails[j]:
            out_refs.pop(0)[...] = pj[tm - tail_rows:, :]


def _proj_call(x, ws, scales, tails, *, tm, keep, pairs):
    s = x.shape[0]
    assert s % tm == 0
    n = s // tm
    tail_rows = min(tm, keep)
    assert keep % tail_rows == 0
    n_tail_blocks = keep // tail_rows
    row_spec = pl.BlockSpec((tm, D_MODEL), lambda i: (i, 0))
    pair_spec = pl.BlockSpec((N_PAIRS, tm, PAIR_W), lambda i: (0, i, 0))
    tail_spec = pl.BlockSpec((tail_rows, D_MODEL),
                             lambda i: (jnp.maximum(i - (n - n_tail_blocks), 0), 0))
    out_shape, out_specs = [], []
    for t in tails:
        if pairs:
            out_shape.append(jax.ShapeDtypeStruct((N_PAIRS, s, PAIR_W), BF16))
            out_specs.append(pair_spec)
        else:
            out_shape.append(jax.ShapeDtypeStruct((s, D_MODEL), BF16))
            out_specs.append(row_spec)
        if t:
            out_shape.append(jax.ShapeDtypeStruct((keep, D_MODEL), F32))
            out_specs.append(tail_spec)
    return pl.pallas_call(
        functools.partial(_proj_kernel, scales=tuple(scales), tails=tuple(tails),
                          tail_rows=tail_rows, pairs=pairs),
        grid=(n,),
        in_specs=[row_spec] + [w.spec() for w in ws],
        out_specs=out_specs,
        out_shape=out_shape,
        compiler_params=_params(),
        name="proj",
    )(x, *[w.array for w in ws])


def _rel_bias_rows(rvec_ref, h, rows, valid):
    row = jnp.broadcast_to(rvec_ref[h:h + 1, :], (rows, REL_RING))
    t = pltpu.roll(row, 0, 1, stride=1, stride_axis=0)
    return jnp.where(valid, t * LOG2E, MASK_VALUE)


def _softmax_weights(s):
    m = jnp.max(s, axis=-1, keepdims=True)
    e = jnp.exp2(s - m)
    return e.astype(BF16), jnp.sum(e, axis=-1, keepdims=True)


def _weighted_values(eb, l, v_parts):
    o = None
    off = 0
    for v in v_parts:
        part = _dot(eb[:, off:off + v.shape[0]], v)
        o = part if o is None else o + part
        off += v.shape[0]
    return o / l


def _attn_prompt_kernel(q_ref, kp_ref, kc_ref, vp_ref, vc_ref, rvec_ref, o_ref, bias_ref):
    step = pl.program_id(0)
    low = lax.broadcasted_iota(jnp.int32, (Q_BLOCK, PAIR_W), 1) < HEAD_DIM
    n_blocks = ATT_STEP // Q_BLOCK
    missing = [max(BAND_PAST - b * Q_BLOCK, 0) for b in range(n_blocks)]

    def build_bias(first_key):
        qi = lax.broadcasted_iota(jnp.int32, (Q_BLOCK, REL_RING), 0)
        kj = lax.broadcasted_iota(jnp.int32, (Q_BLOCK, REL_RING), 1)
        start = (qi >> 6) << 6
        valid = (kj >= start) & (kj < start + (BAND_PAST + CHUNK)) & (kj >= first_key)
        for h in range(N_HEADS):
            bias_ref[h * Q_BLOCK:(h + 1) * Q_BLOCK, :] = (
                _rel_bias_rows(rvec_ref, h, Q_BLOCK, valid)[:, :K_BLOCK])

    for blk in range(n_blocks):
        if blk == 0:
            rebuild = (step == 0) | (step == 1) if missing[-1] else (step == 0)
        else:
            rebuild = (step == 0) if missing[blk] != missing[blk - 1] else None
        if rebuild is not None:
            pl.when(rebuild)(functools.partial(
                build_bias, jnp.where(step == 0, missing[blk], 0)))

        q_rows = slice(blk * Q_BLOCK, (blk + 1) * Q_BLOCK)
        past_lo = blk * Q_BLOCK
        cur_rows = slice(max(past_lo - BAND_PAST, 0), past_lo + Q_BLOCK)

        def window(prev_ref, cur_ref, p):
            parts = [prev_ref[p, past_lo:, :]] if past_lo < BAND_PAST else []
            return parts + [cur_ref[p, cur_rows, :]]

        def scores(p):
            qb = q_ref[p, q_rows, :]
            qs = jnp.concatenate([jnp.where(low, qb, jnp.zeros_like(qb)),
                                  jnp.where(low, jnp.zeros_like(qb), qb)], axis=0)
            s = jnp.concatenate([_dot_nt(qs, k) for k in window(kp_ref, kc_ref, p)], axis=-1)
            return s + bias_ref[2 * p * Q_BLOCK:2 * (p + 1) * Q_BLOCK, :]

        def finish(p, s):
            o = _weighted_values(*_softmax_weights(s), window(vp_ref, vc_ref, p))
            o_ref[p, q_rows, :] = jnp.where(low, o[:Q_BLOCK], o[Q_BLOCK:]).astype(BF16)

        s_next = scores(0)
        for p in range(N_PAIRS):
            s_cur = s_next
            if p + 1 < N_PAIRS:
                s_next = scores(p + 1)
            finish(p, s_cur)


def _attn_prompt_call(q, k, v, rvec):
    s = q.shape[1]
    assert s % ATT_STEP == 0 and ATT_STEP % BAND_PAST == 0
    n = s // ATT_STEP
    ratio = ATT_STEP // BAND_PAST
    cur = pl.BlockSpec((N_PAIRS, ATT_STEP, PAIR_W), lambda i: (0, i, 0))
    prev = pl.BlockSpec((N_PAIRS, BAND_PAST, PAIR_W), lambda i: (0, jnp.maximum(ratio * i - 1, 0), 0))
    return pl.pallas_call(
        _attn_prompt_kernel,
        grid=(n,),
        in_specs=[cur, prev, cur, prev, cur, _const_spec(rvec.shape)],
        out_specs=cur,
        out_shape=jax.ShapeDtypeStruct((N_PAIRS, s, PAIR_W), BF16),
        scratch_shapes=[pltpu.VMEM((N_HEADS * Q_BLOCK, K_BLOCK), F32)],
        compiler_params=_params(),
        name="attn_prompt",
    )(q, k, k, v, v, rvec)


def _attn_decode_kernel(q_ref, kn_ref, vn_ref, kc_ref, vc_ref, rvec_ref, o_ref, bias_ref):
    t = q_ref.shape[0]
    w = kc_ref.shape[1]
    kpad = bias_ref.shape[1]

    @pl.when(pl.program_id(0) == 0)
    def _():
        kj = lax.broadcasted_iota(jnp.int32, (t, REL_RING), 1)
        valid = kj < w + t
        for h in range(N_HEADS):
            bias_ref[h * t:(h + 1) * t, :] = _rel_bias_rows(rvec_ref, h, t, valid)[:, :kpad]

    q = q_ref[...]
    head_of_lane = lax.broadcasted_iota(jnp.int32, (t, D_MODEL), 1) >> 6
    qs = jnp.concatenate(
        [jnp.where(head_of_lane == h, q, jnp.zeros_like(q)) for h in range(N_HEADS)], axis=0)
    zeros = jnp.zeros((kpad - w - t, D_MODEL), BF16)
    kk = jnp.concatenate([kc_ref[0].astype(BF16), kn_ref[...], zeros], axis=0)
    vv = jnp.concatenate([vc_ref[0].astype(BF16), vn_ref[...], zeros], axis=0)
    eb, l = _softmax_weights(_dot_nt(qs, kk) + bias_ref[...])
    o = _weighted_values(eb, l, (vv,))
    out = jnp.zeros((t, D_MODEL), F32)
    for h in range(N_HEADS):
        out = out + jnp.where(head_of_lane == h, o[h * t:(h + 1) * t, :], 0.0)
    o_ref[...] = out.astype(BF16)


def _attn_decode_call(q, kn, vn, cache_k, cache_v, rvec, *, t):
    nb, w = cache_k.shape[0], cache_k.shape[1]
    assert w == BAND_PAST
    kpad = -(-(w + t) // LANES_V7X) * LANES_V7X
    assert kpad + t <= REL_RING
    new_spec = pl.BlockSpec((t, D_MODEL), lambda i: (i, 0))
    cache_spec = pl.BlockSpec((1, w, D_MODEL), lambda i: (i, 0, 0))
    return pl.pallas_call(
        _attn_decode_kernel,
        grid=(nb,),
        in_specs=[new_spec, new_spec, new_spec, cache_spec, cache_spec, _const_spec(rvec.shape)],
        out_specs=new_spec,
        out_shape=jax.ShapeDtypeStruct((nb * t, D_MODEL), BF16),
        scratch_shapes=[pltpu.VMEM((N_HEADS * t, kpad), F32)],
        compiler_params=_params(),
        name="attn_decode",
    )(q, kn, vn, cache_k, cache_v, rvec)


def _rel_vector(table):
    m = jnp.arange(REL_RING)
    d = jnp.where(m < K_BLOCK, m, m - REL_RING)
    idx = jnp.clip(BAND_PAST - d, -REL_MAX, REL_MAX) + REL_MAX
    return table[idx].T.astype(F32)


def _trunk(x, conv_prev, caches, wts, layer_bf16, *, tm, n_sub, proj_tm, keep):
    s = x.shape[0]
    ln = lambda l: (wts["ln_mix_g"][l][None], wts["ln_mix_b"][l][None],
                    wts["ln_ffn_g"][l][None], wts["ln_ffn_b"][l][None])
    stream_len = 0 if conv_prev is None else s // conv_prev.shape[1]
    pairs = caches is None
    convert = layer_bf16 is None
    converted = []
    ztails = []
    for l in range(N_A):
        if convert:
            w_in, w_out = _Weight(wts["w_in_a"], l), _Weight(wts["w_out_a"], l)
            w_up, w_down = _Weight(wts["w_up"], l), _Weight(wts["w_down"], l)
        else:
            w_in, w_out, w_up, w_down = (_Weight(w) for w in layer_bf16[l])
        mix = (w_in, _Weight(wts["conv_w_a"], l), w_out)
        if stream_len:
            p2 = jnp.pad(conv_prev[l], ((0, 0), (0, stream_len - 2), (0, 0))).reshape(s, D_MODEL)
            p1 = jnp.pad(conv_prev[l][:, 1:], ((0, 0), (0, stream_len - 1), (0, 0))).reshape(s, D_MODEL)
            mix = (p1, p2) + mix
        x, zt, *bf16 = _layer_call(x, mix, ln(l), w_up, w_down, conv=True, stream_len=stream_len,
                                   pairs=False, tm=tm, n_sub=n_sub, convert=convert)
        ztails.append(zt)
        converted.append(bf16)

    q_scale = HEAD_DIM ** -0.5 * LOG2E
    q, k, kf, v, vf = _proj_call(
        x, (_Weight(wts["w_q_bf16"], 0), _Weight(wts["w_k_bf16"]), _Weight(wts["w_v_bf16"])),
        (q_scale, 1.0, 1.0), (False, True, True), tm=proj_tm, keep=keep, pairs=pairs)
    n_b = DEPTH - N_A
    for j in range(n_b):
        l = N_A + j
        rvec = _rel_vector(wts["rel_bias_b"][j])
        if caches is None:
            a = _attn_prompt_call(q, k, v, rvec)
        else:
            a = _attn_decode_call(q, k, v, caches[0], caches[1], rvec, t=s // caches[0].shape[0])
        if convert:
            w_o = _Weight(wts["w_o_b"], j)
            w_up, w_down = _Weight(wts["w_up"], l), _Weight(wts["w_down"], l)
        else:
            w_o, w_up, w_down = (_Weight(w) for w in layer_bf16[l])
        fuse_q = pairs and j + 1 < n_b
        outs = _layer_call(x, (a, w_o), ln(l), w_up, w_down,
                           conv=False, stream_len=0, pairs=pairs, tm=tm, n_sub=n_sub,
                           w_q=_Weight(wts["w_q_bf16"], j + 1) if fuse_q else None,
                           q_scale=q_scale if fuse_q else 0.0, convert=convert)
        x = outs[0]
        if fuse_q:
            q = outs[1]
        elif j + 1 < n_b:
            (q,) = _proj_call(x, (_Weight(wts["w_q_bf16"], j + 1),), (q_scale,), (False,),
                              tm=proj_tm, keep=keep, pairs=pairs)
        converted.append(outs[len(outs) - 3:] if convert else [])
    return x, ztails, kf, vf, converted


def kernel(x_prompt, x_sample, cache_conv, cache_k, cache_v, ln_mix_g, ln_mix_b, ln_ffn_g, ln_ffn_b,
           w_up, w_down, w_in_a, conv_w_a, w_out_a, w_k, w_v, w_q_b, w_o_b, rel_bias_b):
    assert x_prompt.shape[0] == 1
    wts = dict(
        ln_mix_g=ln_mix_g, ln_mix_b=ln_mix_b, ln_ffn_g=ln_ffn_g, ln_ffn_b=ln_ffn_b,
        w_up=w_up, w_down=w_down, w_in_a=w_in_a, conv_w_a=conv_w_a, w_out_a=w_out_a, w_o_b=w_o_b,
        w_k_bf16=w_k.astype(BF16), w_v_bf16=w_v.astype(BF16), w_q_bf16=w_q_b.astype(BF16),
        rel_bias_b=rel_bias_b)

    nb, t = x_sample.shape[0], x_sample.shape[1]
    w = cache_k.shape[1]
    ns = nb * t
    ys, zts, kfs, vfs, layer_bf16 = _trunk(
        x_sample.reshape(ns, D_MODEL), cache_conv,
        (cache_k.reshape(nb, w, D_MODEL), cache_v.reshape(nb, w, D_MODEL)),
        wts, None, tm=ns, n_sub=1, proj_tm=ns, keep=ns)
    conv_sample = jnp.stack([zt.reshape(nb, t, D_MODEL)[:, t - 2:] for zt in zts])
    k_sample = kfs.reshape(nb, t, N_HEADS, HEAD_DIM)
    v_sample = vfs.reshape(nb, t, N_HEADS, HEAD_DIM)

    seq = x_prompt.shape[1]
    keep = min(BAND_PAST, seq)
    yp, ztp, kfp, vfp, _ = _trunk(x_prompt[0], None, None, wts, layer_bf16,
                                  tm=LAYER_TILE_ROWS, n_sub=LAYER_SUB_TILES, proj_tm=PROJ_TILE_ROWS,
                                  keep=keep)
    conv_prompt = jnp.stack([zt[SUBLANES_V7X - 2:] for zt in ztp])[:, None]
    k_prompt = kfp.reshape(1, keep, N_HEADS, HEAD_DIM)
    v_prompt = vfp.reshape(1, keep, N_HEADS, HEAD_DIM)
    return (yp[None], ys.reshape(nb, t, D_MODEL), conv_prompt, k_prompt, v_prompt,
            conv_sample, k_sample, v_sample)
```
